```python
import jax, jax.numpy as jnp
from jax import lax
import numpy as np

D_MODEL = 1024
BATCH = 2
SEQ = 16384
DEPTH = 4

N_MIXERS = 2
N_A_LAYERS = (DEPTH + 1) // 2
N_B_LAYERS = DEPTH // 2
RMS_EPS = 1e-6
NEG_INF = -1e30

A_GROUPS = ((128, 1), (512, 4), (2048, 16))
A_N_GROUPS = len(A_GROUPS)
A_HEADS = 16
A_HEAD_DIM = D_MODEL // A_HEADS
A_WIDTH = A_HEADS * A_HEAD_DIM
A_IN_WIDTH = A_N_GROUPS * 3 * A_WIDTH
ROPE_THETA = 10000.0

B_HEADS = 4
B_KEY_DIM = D_MODEL // 2 // B_HEADS
B_VAL_DIM = D_MODEL // B_HEADS
B_QK_WIDTH = B_HEADS * B_KEY_DIM
B_V_WIDTH = B_HEADS * B_VAL_DIM
B_GATE_RANK = 16
B_GATE_TAU = 16.0
B_CHUNK = 64
B_IN_WIDTH = 2 * B_QK_WIDTH + 2 * B_V_WIDTH + 2 * B_GATE_RANK

FFN_HIDDEN = -(-8 * D_MODEL // (3 * 256)) * 256

kernel_name = "hybrid_dilated_attn_gla_encoder"


def rms_norm(x, gain):
    xf = x.astype(jnp.float32)
    y = xf * lax.rsqrt(jnp.mean(xf * xf, axis=-1, keepdims=True) + RMS_EPS)
    return (y * gain.astype(jnp.float32)).astype(x.dtype)


def rope(x, positions):
    half = x.shape[-1] // 2
    inv_freq = ROPE_THETA ** (-jnp.arange(half, dtype=jnp.float32) / half)
    ang = positions.astype(jnp.float32)[:, None] * inv_freq[None, :]
    cos = jnp.cos(ang)[:, None, :]
    sin = jnp.sin(ang)[:, None, :]
    xf = x.astype(jnp.float32)
    x1, x2 = xf[..., :half], xf[..., half:]
    return jnp.concatenate([x1 * cos - x2 * sin, x2 * cos + x1 * sin], axis=-1).astype(x.dtype)


def dilated_window_attention(q, k, v, window, dilation):
    bsz, seq, nh, dh = q.shape
    half = window // (2 * dilation)
    L = seq // dilation
    nb = -(-L // half)
    Lp = nb * half

    def to_phase(t):
        t = t.reshape(bsz, L, dilation, nh, dh)
        return jnp.moveaxis(t, 2, 1).reshape(bsz * dilation, L, nh, dh)

    n = bsz * dilation
    qp = jnp.pad(to_phase(q), ((0, 0), (0, Lp - L), (0, 0), (0, 0))).reshape(n, nb, half, nh, dh)

    def key_blocks(t):
        t = jnp.pad(to_phase(t), ((0, 0), (half, Lp - L + half), (0, 0), (0, 0)))
        t = t.reshape(n, nb + 2, half, nh, dh)
        return jnp.concatenate([t[:, :-2], t[:, 1:-1], t[:, 2:]], axis=2)

    kb = key_blocks(k)
    vb = key_blocks(v)
    tq = jnp.arange(nb)[:, None] * half + jnp.arange(half)[None, :]
    tk = jnp.arange(nb)[:, None] * half + jnp.arange(3 * half)[None, :] - half
    dist = tk[:, None, :] - tq[:, :, None]
    valid = (tk[:, None, :] >= 0) & (tk[:, None, :] < L) & (jnp.abs(dist) <= half)

    scores = jnp.einsum("nbqhd,nbkhd->nbhqk", qp.astype(jnp.float32), kb.astype(jnp.float32)) * (dh ** -0.5)
    scores = jnp.where(valid[None, :, None], scores, NEG_INF)
    m = jnp.max(scores, axis=-1, keepdims=True)
    p = jnp.exp(scores - m)
    l = jnp.sum(p, axis=-1)
    o = jnp.einsum("nbhqk,nbkhd->nbqhd", p, vb.astype(jnp.float32))
    o = o / jnp.moveaxis(l, 2, 3)[..., None]
    lse = jnp.moveaxis(m[..., 0] + jnp.log(l), 2, 3)

    def from_phase(t):
        rest = t.shape[4:]
        t = t.reshape(bsz, dilation, Lp, nh, *rest)[:, :, :L]
        return jnp.moveaxis(t, 1, 2).reshape(bsz, seq, nh, *rest)

    return from_phase(o), from_phase(lse)


def dilated_attention_mixer(h, w_in, q_gain, k_gain, w_out, positions):
    bsz, seq, _ = h.shape
    qkv = (h @ w_in).reshape(bsz, seq, A_N_GROUPS, 3, A_HEADS, A_HEAD_DIM)
    outs, lses = [], []
    for g, (window, dilation) in enumerate(A_GROUPS):
        q = rope(rms_norm(qkv[:, :, g, 0], q_gain[g]), positions)
        k = rope(rms_norm(qkv[:, :, g, 1], k_gain[g]), positions)
        v = qkv[:, :, g, 2]
        o, lse = dilated_window_attention(q, k, v, window, dilation)
        outs.append(o)
        lses.append(lse)
    alpha = jax.nn.softmax(jnp.stack(lses, axis=0), axis=0)
    out = jnp.sum(alpha[..., None] * jnp.stack(outs, axis=0), axis=0)
    return out.reshape(bsz, seq, A_WIDTH).astype(h.dtype) @ w_out


def gla_chunk(q, k, v, log_a, strict):
    bsz, nh, seq, dk = q.shape
    dv = v.shape[-1]
    nc = seq // B_CHUNK
    q = q.astype(jnp.float32).reshape(bsz, nh, nc, B_CHUNK, dk)
    k = k.astype(jnp.float32).reshape(bsz, nh, nc, B_CHUNK, dk)
    v = v.astype(jnp.float32).reshape(bsz, nh, nc, B_CHUNK, dv)
    b = jnp.cumsum(log_a.astype(jnp.float32).reshape(bsz, nh, nc, B_CHUNK, dk), axis=3)
    b_last = b[..., -1:, :]
    q_t = q * jnp.exp(b)
    k_t = k * jnp.exp(-b)
    k_end = k * jnp.exp(b_last - b)
    mask = jnp.tril(jnp.ones((B_CHUNK, B_CHUNK), dtype=bool), k=-1 if strict else 0)
    attn = jnp.where(mask, jnp.einsum("bhncd,bhnsd->bhncs", q_t, k_t), 0.0)
    o_intra = jnp.einsum("bhncs,bhnse->bhnce", attn, v)
    chunk_kv = jnp.einsum("bhncd,bhnce->bhnde", k_end, v)
    decay = jnp.exp(b_last[..., 0, :])

    def step(state, inp):
        kv_n, dec_n = inp
        return dec_n[..., None] * state + kv_n, state

    init = jnp.zeros((bsz, nh, dk, dv), jnp.float32)
    _, s_in = lax.scan(step, init, (jnp.moveaxis(chunk_kv, 2, 0), jnp.moveaxis(decay, 2, 0)))
    s_in = jnp.moveaxis(s_in, 0, 2)
    o_inter = jnp.einsum("bhncd,bhnde->bhnce", q_t, s_in)
    return (o_intra + o_inter).reshape(bsz, nh, seq, dv)


def _heads(t, nh):
    bsz, seq, _ = t.shape
    return t.reshape(bsz, seq, nh, -1).transpose(0, 2, 1, 3)


def gla_mixer(h, w_in, w_gate_f, bias_gate_f, w_gate_b, bias_gate_b, out_gain, w_out):
    bsz, seq, _ = h.shape
    proj = h @ w_in
    cuts = np.cumsum([B_QK_WIDTH, B_QK_WIDTH, B_V_WIDTH, B_V_WIDTH, B_GATE_RANK]).tolist()
    q, k, v, r, zf, zb = jnp.split(proj, cuts, axis=-1)
    q = _heads(q, B_HEADS) * (B_KEY_DIM ** -0.5)
    k = _heads(k, B_HEADS)
    v = _heads(v, B_HEADS)
    log_af = jax.nn.log_sigmoid((zf @ w_gate_f + bias_gate_f).astype(jnp.float32)) / B_GATE_TAU
    log_ab = jax.nn.log_sigmoid((zb @ w_gate_b + bias_gate_b).astype(jnp.float32)) / B_GATE_TAU
    log_af = _heads(log_af, B_HEADS)
    log_ab = _heads(log_ab, B_HEADS)
    o_f = gla_chunk(q, k, v, log_af, strict=False)
    flip = lambda t: jnp.flip(t, axis=2)
    o_b = flip(gla_chunk(flip(q), flip(k), flip(v), flip(log_ab), strict=True))
    o = (o_f + o_b).transpose(0, 2, 1, 3)
    o = rms_norm(o, out_gain).reshape(bsz, seq, B_V_WIDTH)
    o = o * jax.nn.silu(r.astype(jnp.float32))
    return o.astype(h.dtype) @ w_out


def swiglu(h, w_gate_up, w_down):
    g, u = jnp.split(h @ w_gate_up, 2, axis=-1)
    return (jax.nn.silu(g) * u) @ w_down


def setup_inputs(seed: int = 0) -> dict:
    key = jax.random.key(seed)
    ks = jax.random.split(key, 16)

    def nrm(k, shape, scale):
        return jax.random.normal(k, shape, jnp.float32) * scale

    return {
        "x": nrm(ks[0], (BATCH, SEQ, D_MODEL), 1.0),
        "attn_norm": 1.0 + nrm(ks[1], (DEPTH, D_MODEL), 0.02),
        "ffn_norm": 1.0 + nrm(ks[2], (DEPTH, D_MODEL), 0.02),
        "a_w_in": nrm(ks[3], (N_A_LAYERS, D_MODEL, A_IN_WIDTH), D_MODEL ** -0.5),
        "a_q_norm": 1.0 + nrm(ks[4], (N_A_LAYERS, A_N_GROUPS, A_HEAD_DIM), 0.02),
        "a_k_norm": 1.0 + nrm(ks[5], (N_A_LAYERS, A_N_GROUPS, A_HEAD_DIM), 0.02),
        "a_w_out": nrm(ks[6], (N_A_LAYERS, A_WIDTH, D_MODEL), A_WIDTH ** -0.5),
        "b_w_in": nrm(ks[7], (N_B_LAYERS, D_MODEL, B_IN_WIDTH), D_MODEL ** -0.5),
        "b_w_gate_f": nrm(ks[8], (N_B_LAYERS, B_GATE_RANK, B_QK_WIDTH), B_GATE_RANK ** -0.5),
        "b_gate_bias_f": nrm(ks[9], (N_B_LAYERS, B_QK_WIDTH), 0.1),
        "b_w_gate_b": nrm(ks[10], (N_B_LAYERS, B_GATE_RANK, B_QK_WIDTH), B_GATE_RANK ** -0.5),
        "b_gate_bias_b": nrm(ks[11], (N_B_LAYERS, B_QK_WIDTH), 0.1),
        "b_out_norm": 1.0 + nrm(ks[12], (N_B_LAYERS, B_HEADS, B_VAL_DIM), 0.02),
        "b_w_out": nrm(ks[13], (N_B_LAYERS, B_V_WIDTH, D_MODEL), B_V_WIDTH ** -0.5),
        "ffn_w_gate_up": nrm(ks[14], (DEPTH, D_MODEL, 2 * FFN_HIDDEN), D_MODEL ** -0.5),
        "ffn_w_down": nrm(ks[15], (DEPTH, FFN_HIDDEN, D_MODEL), FFN_HIDDEN ** -0.5),
    }


def reference(x, attn_norm, ffn_norm, a_w_in, a_q_norm, a_k_norm, a_w_out, b_w_in, b_w_gate_f, b_gate_bias_f, b_w_gate_b, b_gate_bias_b, b_out_norm, b_w_out, ffn_w_gate_up, ffn_w_down):
    positions = jnp.arange(x.shape[1])
    h = x
    for i in range(DEPTH):
        j = i // N_MIXERS
        hn = rms_norm(h, attn_norm[i])
        if i % N_MIXERS == 0:
            mix = dilated_attention_mixer(hn, a_w_in[j], a_q_norm[j], a_k_norm[j], a_w_out[j], positions)
        else:
            mix = gla_mixer(hn, b_w_in[j], b_w_gate_f[j], b_gate_bias_f[j], b_w_gate_b[j], b_gate_bias_b[j], b_out_norm[j], b_w_out[j])
        h = h + mix.astype(h.dtype)
        h = h + swiglu(rms_norm(h, ffn_norm[i]), ffn_w_gate_up[i], ffn_w_down[i]).astype(h.dtype)
    return h
```

```python
import functools

import numpy as np
import jax
import jax.numpy as jnp
from jax import lax
from jax.experimental import pallas as pl
from jax.experimental.pallas import tpu as pltpu

F32 = jnp.float32
BF16 = jnp.bfloat16

D_MODEL = 1024
DEPTH = 4
RMS_EPS = 1e-6
NEG_INF = -1e30

A_GROUPS = ((128, 1), (512, 4), (2048, 16))
A_HEADS = 16
A_HEAD_DIM = 64
A_HALF = 64
ROPE_THETA = 10000.0

B_HEADS = 4
B_KEY_DIM = 128
B_VAL_DIM = 256
B_QK_WIDTH = B_HEADS * B_KEY_DIM
B_V_WIDTH = B_HEADS * B_VAL_DIM
B_GATE_RANK = 16
B_GATE_TAU = 16.0
B_CHUNK = 64

FFN_HIDDEN = 2816

LANES = 128
V7X_VMEM_BYTES = 64 * 1024 * 1024
VMEM_LIMIT_BYTES = V7X_VMEM_BYTES - 8 * 1024 * 1024

Q_TILE = 128
K_TILE = Q_TILE + 2 * A_HALF
HEADS_PER_SLAB = 4
SLAB = HEADS_PER_SLAB * A_HEAD_DIM
PHASE_BLOCK = 256

for _w, _d in A_GROUPS:
    assert _w // (2 * _d) == A_HALF


def _cparams(sem):
    return pltpu.CompilerParams(dimension_semantics=sem, vmem_limit_bytes=VMEM_LIMIT_BYTES)


def _nt_dot(a, b):
    return lax.dot_general(a, b, (((1,), (1,)), ((), ())), preferred_element_type=F32)


def _dot(a, b):
    return jnp.dot(a, b, preferred_element_type=F32)


def _rms_rows(x, gain):
    ms = jnp.mean(x * x, axis=-1, keepdims=True)
    return x * lax.rsqrt(ms + RMS_EPS) * gain


def _a_in_kernel(h_ref, ng_ref, w_ref, qkg_ref, cos_ref, sin_ref, seg_ref, perm_ref, out_ref, xn_ref, xnat_ref,
                 *, tm):
    j = pl.program_id(1)
    sub = PHASE_BLOCK

    @pl.when(j == 0)
    def _():
        xn = _rms_rows(h_ref[...], ng_ref[...]).astype(BF16)
        xn_ref[...] = xn
        xnat_ref[...] = xn

    for g in range(1, len(A_GROUPS)):
        @pl.when(j == 3 * g)
        def _(g=g):
            for s in range(tm // sub):
                rows = slice(s * sub, (s + 1) * sub)
                xn_ref[rows, :] = _dot(perm_ref[g - 1], xnat_ref[rows, :]).astype(BF16)

    kind = j % 3

    @pl.when(kind == 2)
    def _():
        for s in range(tm // sub):
            rows = slice(s * sub, (s + 1) * sub)
            out_ref[rows, :] = _dot(xn_ref[rows, :], w_ref[...]).astype(BF16)

    @pl.when(kind != 2)
    def _():
        scale = jnp.where(kind == 0, A_HEAD_DIM ** -0.5, 1.0).astype(F32)
        for s in range(tm // sub):
            rows = slice(s * sub, (s + 1) * sub)
            acc = _dot(xn_ref[rows, :], w_ref[...])
            cos = cos_ref[rows, :]
            sin = sin_ref[rows, :]
            for g4 in range(D_MODEL // SLAB):
                c0 = g4 * SLAB
                a = acc[:, c0:c0 + LANES]
                b = acc[:, c0 + LANES:c0 + SLAB]
                ss = _dot((a * a + b * b).astype(BF16), seg_ref[...])
                r = lax.rsqrt(ss * (1.0 / A_HEAD_DIM) + RMS_EPS) * scale
                an = a * r * qkg_ref[:, c0:c0 + LANES]
                bn = b * r * qkg_ref[:, c0 + LANES:c0 + SLAB]
                out_ref[rows, c0:c0 + LANES] = (an * cos - bn * sin).astype(BF16)
                out_ref[rows, c0 + LANES:c0 + SLAB] = (bn * cos + an * sin).astype(BF16)


def _a_in_proj(h, norm_gain, w_perm, qk_gain, cos_t, sin_t, seg, perms, *, seq, tm=1024):
    tokens = h.shape[0]
    n_chunks = w_perm.shape[1] // D_MODEL
    tm = min(tm, seq)
    pos_blocks = seq // tm
    table = pl.BlockSpec((None, tm, LANES), lambda i, j: (j // 3, i % pos_blocks, 0))
    return pl.pallas_call(
        functools.partial(_a_in_kernel, tm=tm),
        grid=(tokens // tm, n_chunks),
        in_specs=[
            pl.BlockSpec((tm, D_MODEL), lambda i, j: (i, 0)),
            pl.BlockSpec((1, D_MODEL), lambda i, j: (0, 0)),
            pl.BlockSpec((D_MODEL, D_MODEL), lambda i, j: (0, j)),
            pl.BlockSpec((1, D_MODEL), lambda i, j: (0, j)),
            table,
            table,
            pl.BlockSpec((LANES, LANES), lambda i, j: (0, 0)),
            pl.BlockSpec(perms.shape, lambda i, j: (0, 0, 0)),
        ],
        out_specs=pl.BlockSpec((tm, D_MODEL), lambda i, j: (i, j)),
        out_shape=jax.ShapeDtypeStruct((tokens, n_chunks * D_MODEL), BF16),
        scratch_shapes=[pltpu.VMEM((tm, D_MODEL), BF16), pltpu.VMEM((tm, D_MODEL), BF16)],
        compiler_params=_cparams(("parallel", "arbitrary")),
        name="a_in_proj",
    )(h, norm_gain, w_perm, qk_gain, cos_t, sin_t, seg, perms)


def _attn_kernel(q_ref, kp_ref, kc_ref, kn_ref, vp_ref, vc_ref, vn_ref, o_ref, lse_ref,
                 q2_ref, kcat_ref, vcat_ref, o2_ref, lse2_ref, *, tq, phase_len):
    i = pl.program_id(2)
    flat = lambda ref: ref[...].reshape(-1, ref.shape[-1])
    q2_ref[...] = flat(q_ref)
    kcat_ref[0:A_HALF, :] = flat(kp_ref)
    kcat_ref[A_HALF:A_HALF + tq, :] = flat(kc_ref)
    kcat_ref[A_HALF + tq:2 * A_HALF + tq, :] = flat(kn_ref)
    vcat_ref[0:A_HALF, :] = flat(vp_ref)
    vcat_ref[A_HALF:A_HALF + tq, :] = flat(vc_ref)
    vcat_ref[A_HALF + tq:2 * A_HALF + tq, :] = flat(vn_ref)

    row = lax.broadcasted_iota(jnp.int32, (Q_TILE, K_TILE), 0)
    col = lax.broadcasted_iota(jnp.int32, (Q_TILE, K_TILE), 1)
    band = (col >= row) & (col <= row + 2 * A_HALF)
    q_head = (col & (LANES - 1)) >> 5
    q_keep = [(q_head == hi).astype(F32).astype(BF16) for hi in range(HEADS_PER_SLAB)]
    v_head = col >> 6
    lse_col = lax.broadcasted_iota(jnp.int32, (Q_TILE, LANES), 1)

    def tile(s, carry):
        r0 = pl.multiple_of(s * Q_TILE, Q_TILE)
        key_pos = i * tq + s * Q_TILE - A_HALF + col
        valid = band & (key_pos >= 0) & (key_pos < phase_len)
        lse_blk = jnp.zeros((Q_TILE, LANES), F32)
        for hg in range(D_MODEL // SLAB):
            cs = slice(hg * SLAB, (hg + 1) * SLAB)
            q4 = q2_ref[pl.ds(r0, Q_TILE), cs]
            k4 = kcat_ref[pl.ds(r0, K_TILE), cs]
            v4 = vcat_ref[pl.ds(r0, K_TILE), cs]
            o4 = jnp.zeros((Q_TILE, SLAB), F32)
            for hi in range(HEADS_PER_SLAB):
                qm = q4 * q_keep[hi]
                sc = jnp.where(valid, _nt_dot(qm, k4), NEG_INF)
                m = jnp.max(sc, axis=-1, keepdims=True)
                p = jnp.exp(sc - m)
                l = jnp.sum(p, axis=-1, keepdims=True)
                pv = _dot(p.astype(BF16), v4)
                o4 = jnp.where(v_head == hi, pv * (1.0 / l), o4)
                lse_blk = jnp.where(lse_col == hg * HEADS_PER_SLAB + hi, m + jnp.log(l), lse_blk)
            o2_ref[pl.ds(r0, Q_TILE), cs] = o4.astype(BF16)
        lse2_ref[pl.ds(r0, Q_TILE), :] = lse_blk
        return carry

    lax.fori_loop(0, tq // Q_TILE, tile, 0)
    o_ref[...] = o2_ref[...].reshape(o_ref.shape)
    lse_ref[...] = lse2_ref[...].reshape(lse_ref.shape)


def _attn_group(qkv, group, *, batch, seq, tq=512):
    dil = A_GROUPS[group][1]
    tokens, width = qkv.shape
    phase_len = seq // dil
    tq = min(tq, phase_len)
    nq = phase_len // tq
    hb = phase_len // A_HALF
    r = tq // A_HALF
    run = PHASE_BLOCK // dil
    cq = group * 3
    cur_i = lambda b, i: b * nq + i
    prev_i = lambda b, i: b * hb + jnp.maximum(i * r - 1, 0)
    next_i = lambda b, i: b * hb + jnp.minimum((i + 1) * r, hb - 1)

    if dil == 1:
        view_shape = lambda cols: (tokens, cols)
        spec = lambda rows, cols, row_i, col_i: pl.BlockSpec(
            (rows, cols), lambda b, p, i: (row_i(b, i), col_i))
    else:
        view_shape = lambda cols: (tokens // PHASE_BLOCK, dil, run, cols)
        spec = lambda rows, cols, row_i, col_i: pl.BlockSpec(
            (rows // run, None, run, cols), lambda b, p, i: (row_i(b, i), p, 0, col_i))
    cur = lambda c: spec(tq, D_MODEL, cur_i, c)
    prev = lambda c: spec(A_HALF, D_MODEL, prev_i, c)
    nxt = lambda c: spec(A_HALF, D_MODEL, next_i, c)

    qkv_v = qkv.reshape(view_shape(width))
    o, lse = pl.pallas_call(
        functools.partial(_attn_kernel, tq=tq, phase_len=phase_len),
        grid=(batch, dil, nq),
        in_specs=[cur(cq), prev(cq + 1), cur(cq + 1), nxt(cq + 1), prev(cq + 2), cur(cq + 2), nxt(cq + 2)],
        out_specs=[spec(tq, D_MODEL, cur_i, 0), spec(tq, LANES, cur_i, 0)],
        out_shape=[
            jax.ShapeDtypeStruct(view_shape(D_MODEL), BF16),
            jax.ShapeDtypeStruct(view_shape(LANES), F32),
        ],
        scratch_shapes=[pltpu.VMEM((tq, D_MODEL), BF16),
                        pltpu.VMEM((tq + 2 * A_HALF, D_MODEL), BF16),
                        pltpu.VMEM((tq + 2 * A_HALF, D_MODEL), BF16),
                        pltpu.VMEM((tq, D_MODEL), BF16),
                        pltpu.VMEM((tq, LANES), F32)],
        compiler_params=_cparams(("parallel", "parallel", "parallel")),
        name=f"attn_group{group}",
    )(qkv_v, qkv_v, qkv_v, qkv_v, qkv_v, qkv_v, qkv_v)
    return o.reshape(tokens, D_MODEL), lse.reshape(tokens, LANES)


def _split3(x):
    hi = x.astype(BF16)
    r1 = x - hi.astype(F32)
    mid = r1.astype(BF16)
    return hi, mid, (r1 - mid.astype(F32)).astype(BF16)


def _a_out_kernel(o0_ref, o1_ref, o2_ref, l0_ref, l1_ref, l2_ref, exp_ref, unperm_ref, w_ref, h_ref, out_ref,
                  *, tm):
    for s in range(tm // PHASE_BLOCK):
        rows = slice(s * PHASE_BLOCK, (s + 1) * PHASE_BLOCK)
        outs = [o0_ref[rows, :].astype(F32)]
        lses = [l0_ref[rows, :]]
        for g, (o_ref, l_ref) in enumerate(((o1_ref, l1_ref), (o2_ref, l2_ref))):
            outs.append(_dot(unperm_ref[g], o_ref[rows, :]))
            lses.append(sum(_dot(unperm_ref[g], t) for t in _split3(l_ref[rows, :])))
        m = jnp.maximum(jnp.maximum(lses[0], lses[1]), lses[2])
        es = [jnp.exp(x - m) for x in lses]
        inv = 1.0 / (es[0] + es[1] + es[2])
        mix = None
        for e, o in zip(es, outs):
            alpha = e * inv
            hi = alpha.astype(BF16)
            lo = (alpha - hi.astype(F32)).astype(BF16)
            wide = _dot(jnp.concatenate([hi, lo], axis=1), exp_ref[...])
            mix = wide * o if mix is None else mix + wide * o
        out_ref[rows, :] = h_ref[rows, :] + _dot(mix.astype(BF16), w_ref[...])


def _a_out_proj(os_, lses, expand, unperms, w_out, h, *, tm=512):
    tokens = h.shape[0]
    row = lambda i: (i, 0)
    const = lambda i: (0, 0)
    return pl.pallas_call(
        functools.partial(_a_out_kernel, tm=tm),
        grid=(tokens // tm,),
        in_specs=[pl.BlockSpec((tm, D_MODEL), row)] * 3 + [pl.BlockSpec((tm, LANES), row)] * 3 + [
            pl.BlockSpec((2 * LANES, D_MODEL), const),
            pl.BlockSpec(unperms.shape, lambda i: (0, 0, 0)),
            pl.BlockSpec((D_MODEL, D_MODEL), const),
            pl.BlockSpec((tm, D_MODEL), row),
        ],
        out_specs=pl.BlockSpec((tm, D_MODEL), row),
        out_shape=jax.ShapeDtypeStruct((tokens, D_MODEL), F32),
        compiler_params=_cparams(("parallel",)),
        name="a_out_proj",
    )(*os_, *lses, expand, unperms, w_out, h)


def _ffn_kernel(h_ref, g_ref, wgu_ref, wd_ref, out_ref, *, n_chunks, th):
    x = h_ref[...]
    xn = _rms_rows(x, g_ref[...]).astype(BF16)
    acc = x
    for c in range(n_chunks):
        gu = _dot(xn, wgu_ref[c])
        gate = gu[:, :th]
        up = gu[:, th:]
        act = (gate * jax.nn.sigmoid(gate) * up).astype(BF16)
        acc = acc + _dot(act, wd_ref[c])
    out_ref[...] = acc


def _ffn(h, gain, wgu, wd, *, tm=512):
    tokens = h.shape[0]
    n_chunks, _, th2 = wgu.shape
    th = th2 // 2
    return pl.pallas_call(
        functools.partial(_ffn_kernel, n_chunks=n_chunks, th=th),
        grid=(tokens // tm,),
        in_specs=[
            pl.BlockSpec((tm, D_MODEL), lambda i: (i, 0)),
            pl.BlockSpec((1, D_MODEL), lambda i: (0, 0)),
            pl.BlockSpec((n_chunks, D_MODEL, th2), lambda i: (0, 0, 0)),
            pl.BlockSpec((n_chunks, th, D_MODEL), lambda i: (0, 0, 0)),
        ],
        out_specs=pl.BlockSpec((tm, D_MODEL), lambda i: (i, 0)),
        out_shape=jax.ShapeDtypeStruct((tokens, D_MODEL), F32),
        compiler_params=_cparams(("parallel",)),
        name="ffn",
    )(h, gain, wgu, wd)


def _log_sigmoid(x):
    return jnp.minimum(x, 0.0) - jnp.log1p(jnp.exp(-jnp.abs(x)))


def _b_in_kernel(h_ref, ng_ref, wq_ref, wk_ref, wv_ref, wvt_ref, wr_ref, wz_ref, wgf_ref, wgb_ref,
                 bf_ref, bb_ref, q_ref, k_ref, v_ref, vt_ref, sr_ref, laf_ref, lab_ref):
    xn = _rms_rows(h_ref[...], ng_ref[...]).astype(BF16)
    q_ref[...] = (_dot(xn, wq_ref[...]) * (B_KEY_DIM ** -0.5)).astype(BF16)
    k_ref[...] = _dot(xn, wk_ref[...]).astype(BF16)
    v_ref[...] = _dot(xn, wv_ref[...]).astype(BF16)
    vt_ref[...] = _nt_dot(wvt_ref[...], xn).astype(BF16)
    r = _dot(xn, wr_ref[...])
    sr_ref[...] = (r * jax.nn.sigmoid(r)).astype(BF16)
    z = _dot(xn, wz_ref[...]).astype(BF16)
    laf_ref[...] = _log_sigmoid(_dot(z, wgf_ref[...]) + bf_ref[...]) * (1.0 / B_GATE_TAU)
    lab_ref[...] = _log_sigmoid(_dot(z, wgb_ref[...]) + bb_ref[...]) * (1.0 / B_GATE_TAU)


def _b_in_proj(h, norm_gain, wq, wk, wv, wvt, wr, wz, wgf, wgb, bias_f, bias_b, *, tm=512):
    tokens = h.shape[0]
    row = lambda i: (i, 0)
    const = lambda i: (0, 0)
    full = lambda a: pl.BlockSpec(a.shape, const)
    return pl.pallas_call(
        _b_in_kernel,
        grid=(tokens // tm,),
        in_specs=[pl.BlockSpec((tm, D_MODEL), row)] + [full(a) for a in
                  (norm_gain, wq, wk, wv, wvt, wr, wz, wgf, wgb, bias_f, bias_b)],
        out_specs=[
            pl.BlockSpec((tm, B_QK_WIDTH), row),
            pl.BlockSpec((tm, B_QK_WIDTH), row),
            pl.BlockSpec((tm, B_V_WIDTH), row),
            pl.BlockSpec((B_V_WIDTH, tm), lambda i: (0, i)),
            pl.BlockSpec((tm, B_V_WIDTH), row),
            pl.BlockSpec((tm, B_QK_WIDTH), row),
            pl.BlockSpec((tm, B_QK_WIDTH), row),
        ],
        out_shape=[
            jax.ShapeDtypeStruct((tokens, B_QK_WIDTH), BF16),
            jax.ShapeDtypeStruct((tokens, B_QK_WIDTH), BF16),
            jax.ShapeDtypeStruct((tokens, B_V_WIDTH), BF16),
            jax.ShapeDtypeStruct((B_V_WIDTH, tokens), BF16),
            jax.ShapeDtypeStruct((tokens, B_V_WIDTH), BF16),
            jax.ShapeDtypeStruct((tokens, B_QK_WIDTH), F32),
            jax.ShapeDtypeStruct((tokens, B_QK_WIDTH), F32),
        ],
        compiler_params=_cparams(("parallel",)),
        name="b_in_proj",
    )(h, norm_gain, wq, wk, wv, wvt, wr, wz, wgf, wgb, bias_f, bias_b)


def _gla_chunks(q_ref, k_ref, v_ref, vt_ref, la_ref, tri_ref, st_ref, write_o, *, cb, reverse):
    n_ch = cb // B_CHUNK
    row = lax.broadcasted_iota(jnp.int32, (B_CHUNK, B_CHUNK), 0)
    col = lax.broadcasted_iota(jnp.int32, (B_CHUNK, B_CHUNK), 1)
    amask = (col > row) if reverse else (col <= row)
    zeros_half = jnp.zeros((B_CHUNK, B_QK_WIDTH), BF16)
    for step in range(n_ch):
        c = n_ch - 1 - step if reverse else step
        rows = slice(c * B_CHUNK, (c + 1) * B_CHUNK)
        g = la_ref[rows, :]
        g_hi = g.astype(BF16)
        g_lo = (g - g_hi.astype(F32)).astype(BF16)
        b = _dot(tri_ref[...], g_hi) + _dot(tri_ref[...], g_lo)
        b_last = b[0:1, :] if reverse else b[B_CHUNK - 1:B_CHUNK, :]
        dec = jnp.exp(b_last)
        qt = (q_ref[rows, :].astype(F32) * jnp.exp(b)).astype(BF16)
        kt_f = k_ref[rows, :].astype(F32) * jnp.exp(-b)
        kt = kt_f.astype(BF16)
        ke = (kt_f * dec).astype(BF16)
        pair = c // 2
        ke_pair = jnp.concatenate([ke, zeros_half] if c % 2 == 0 else [zeros_half, ke], axis=0)
        for hh in range(B_HEADS):
            sk = slice(hh * B_KEY_DIM, (hh + 1) * B_KEY_DIM)
            sv = slice(hh * B_VAL_DIM, (hh + 1) * B_VAL_DIM)
            att = jnp.where(amask, _nt_dot(qt[:, sk], kt[:, sk]), 0.0)
            st = st_ref[hh]
            o = _dot(att.astype(BF16), v_ref[rows, sv]) + _nt_dot(qt[:, sk], st.astype(BF16))
            write_o(rows, sv, o)
            vt = vt_ref[sv, pair * 2 * B_CHUNK:(pair + 1) * 2 * B_CHUNK]
            st_ref[hh] = st * dec[:, sk] + _dot(vt, ke_pair[:, sk])


def _gla_fwd_kernel(q_ref, k_ref, v_ref, vt_ref, la_ref, tri_ref, o_ref, st_ref, *, cb):
    @pl.when(pl.program_id(1) == 0)
    def _():
        st_ref[...] = jnp.zeros_like(st_ref)

    def write_o(rows, sv, o):
        o_ref[rows, sv] = o

    _gla_chunks(q_ref, k_ref, v_ref, vt_ref, la_ref, tri_ref, st_ref, write_o, cb=cb, reverse=False)


def _gla_bwd_out_kernel(q_ref, k_ref, v_ref, vt_ref, la_ref, tri_ref, of_ref, sr_ref, og_ref, w_ref,
                        h_ref, out_ref, st_ref, ob_ref, *, cb):
    @pl.when(pl.program_id(1) == 0)
    def _():
        st_ref[...] = jnp.zeros_like(st_ref)

    def write_o(rows, sv, o):
        ob_ref[rows, sv] = o

    _gla_chunks(q_ref, k_ref, v_ref, vt_ref, la_ref, tri_ref, st_ref, write_o, cb=cb, reverse=True)
    ys = []
    for hh in range(B_HEADS):
        sv = slice(hh * B_VAL_DIM, (hh + 1) * B_VAL_DIM)
        o = of_ref[:, sv] + ob_ref[:, sv]
        ys.append((_rms_rows(o, og_ref[:, sv]) * sr_ref[:, sv].astype(F32)).astype(BF16))
    out_ref[...] = h_ref[...] + _dot(jnp.concatenate(ys, axis=1), w_ref[...])


def _gla_specs(batch, seq, cb, reverse):
    nb = seq // cb
    blk = (lambda b, i: b * nb + (nb - 1 - i)) if reverse else (lambda b, i: b * nb + i)
    row = lambda width: pl.BlockSpec((cb, width), lambda b, i: (blk(b, i), 0))
    const = lambda shape: pl.BlockSpec(shape, lambda b, i: (0,) * len(shape))
    vt = pl.BlockSpec((B_V_WIDTH, cb), lambda b, i: (0, blk(b, i)))
    return nb, row, const, vt


def _gla_fwd(q, k, v, vt, la, tri, *, batch, seq, cb=512):
    cb = min(cb, seq)
    nb, row, const, vt_spec = _gla_specs(batch, seq, cb, False)
    return pl.pallas_call(
        functools.partial(_gla_fwd_kernel, cb=cb),
        grid=(batch, nb),
        in_specs=[row(B_QK_WIDTH), row(B_QK_WIDTH), row(B_V_WIDTH), vt_spec, row(B_QK_WIDTH),
                  const((B_CHUNK, B_CHUNK))],
        out_specs=row(B_V_WIDTH),
        out_shape=jax.ShapeDtypeStruct((batch * seq, B_V_WIDTH), F32),
        scratch_shapes=[pltpu.VMEM((B_HEADS, B_VAL_DIM, B_KEY_DIM), F32)],
        compiler_params=_cparams(("parallel", "arbitrary")),
        name="gla_fwd",
    )(q, k, v, vt, la, tri)


def _gla_bwd_out(q, k, v, vt, la, tri, o_f, silu_r, out_gain, w_out, h, *, batch, seq, cb=512):
    cb = min(cb, seq)
    nb, row, const, vt_spec = _gla_specs(batch, seq, cb, True)
    return pl.pallas_call(
        functools.partial(_gla_bwd_out_kernel, cb=cb),
        grid=(batch, nb),
        in_specs=[row(B_QK_WIDTH), row(B_QK_WIDTH), row(B_V_WIDTH), vt_spec, row(B_QK_WIDTH),
                  const((B_CHUNK, B_CHUNK)), row(B_V_WIDTH), row(B_V_WIDTH), const((1, B_V_WIDTH)),
                  const((B_V_WIDTH, D_MODEL)), row(D_MODEL)],
        out_specs=row(D_MODEL),
        out_shape=jax.ShapeDtypeStruct((batch * seq, D_MODEL), F32),
        scratch_shapes=[pltpu.VMEM((B_HEADS, B_VAL_DIM, B_KEY_DIM), F32),
                        pltpu.VMEM((cb, B_V_WIDTH), F32)],
        compiler_params=_cparams(("parallel", "arbitrary")),
        name="gla_bwd_out",
    )(q, k, v, vt, la, tri, o_f, silu_r, out_gain, w_out, h)


def _a_layer_params(w_in, q_gain, k_gain):
    n_g, n_slab, half = len(A_GROUPS), D_MODEL // SLAB, A_HEAD_DIM // 2
    w7 = w_in.reshape(D_MODEL, n_g, 3, n_slab, HEADS_PER_SLAB, 2, half)
    w_qk = w7[:, :, :2].transpose(0, 1, 2, 3, 5, 4, 6).reshape(D_MODEL, n_g, 2, D_MODEL)
    w_v = w7[:, :, 2:].reshape(D_MODEL, n_g, 1, D_MODEL)
    w_perm = jnp.concatenate([w_qk, w_v], axis=2).reshape(D_MODEL, n_g * 3 * D_MODEL).astype(BF16)

    def spread(gain):
        g5 = jnp.broadcast_to(gain.reshape(n_g, 1, 2, 1, half), (n_g, n_slab, 2, HEADS_PER_SLAB, half))
        return g5.reshape(n_g, 1, D_MODEL)

    gains = jnp.concatenate([spread(q_gain), spread(k_gain), jnp.ones((n_g, 1, D_MODEL), F32)], axis=1)
    return w_perm, gains.reshape(1, n_g * 3 * D_MODEL).astype(F32)


def _phase_major_index(dil):
    r = np.arange(PHASE_BLOCK)
    run = PHASE_BLOCK // dil
    return (r % run) * dil + r // run


def _rope_tables(seq):
    half = A_HEAD_DIM // 2
    inv_freq = ROPE_THETA ** (-jnp.arange(half, dtype=F32) / half)
    blocks = np.arange(seq // PHASE_BLOCK)[:, None] * PHASE_BLOCK
    pos = np.stack([(blocks + _phase_major_index(d)[None, :]).reshape(seq) for _, d in A_GROUPS])
    ang = jnp.asarray(pos, F32)[:, :, None] * inv_freq[None, None, :]
    reps = LANES // half
    return jnp.tile(jnp.cos(ang), (1, 1, reps)), jnp.tile(jnp.sin(ang), (1, 1, reps))


def _phase_perms():
    fwd = np.stack([np.eye(PHASE_BLOCK)[_phase_major_index(d)] for _, d in A_GROUPS[1:]])
    return jnp.asarray(fwd, BF16), jnp.asarray(fwd.transpose(0, 2, 1), BF16)


def _segment_ones():
    a = np.arange(LANES)
    return jnp.asarray((a[:, None] // 32) == (a[None, :] // 32), BF16)


def _head_expand():
    a = np.arange(2 * LANES) % LANES
    n = np.arange(D_MODEL) // A_HEAD_DIM
    return jnp.asarray(a[:, None] == n[None, :], BF16)


def _tri(reverse):
    r = np.arange(B_CHUNK)
    m = (r[None, :] >= r[:, None]) if reverse else (r[None, :] <= r[:, None])
    return jnp.asarray(m, BF16)


def _ffn_params(w_gate_up, w_down, th=256):
    n_chunks = FFN_HIDDEN // th
    wg = w_gate_up[:, :FFN_HIDDEN].reshape(D_MODEL, n_chunks, th)
    wu = w_gate_up[:, FFN_HIDDEN:].reshape(D_MODEL, n_chunks, th)
    wgu = jnp.concatenate([wg, wu], axis=2).transpose(1, 0, 2).astype(BF16)
    return wgu, w_down.reshape(n_chunks, th, D_MODEL).astype(BF16)


def _b_layer_params(w_in, w_gate_f, w_gate_b):
    cuts = np.cumsum([B_QK_WIDTH, B_QK_WIDTH, B_V_WIDTH, B_V_WIDTH]).tolist()
    wq, wk, wv, wr, wz = jnp.split(w_in, cuts, axis=1)
    wz = jnp.pad(wz, ((0, 0), (0, LANES - 2 * B_GATE_RANK)))
    wgf = jnp.pad(w_gate_f, ((0, LANES - B_GATE_RANK), (0, 0)))
    wgb = jnp.pad(w_gate_b, ((B_GATE_RANK, LANES - 2 * B_GATE_RANK), (0, 0)))
    cast = lambda a: a.astype(BF16)
    return cast(wq), cast(wk), cast(wv), cast(wv.T), cast(wr), cast(wz), cast(wgf), cast(wgb)


def _mixer_a(h, norm_gain, w_in, q_gain, k_gain, w_out, tables, *, batch, seq):
    cos_t, sin_t, seg, expand, perms, unperms = tables
    w_perm, qk_gain = _a_layer_params(w_in, q_gain, k_gain)
    qkv = _a_in_proj(h, norm_gain, w_perm, qk_gain, cos_t, sin_t, seg, perms, seq=seq)
    outs = [_attn_group(qkv, g, batch=batch, seq=seq) for g in range(len(A_GROUPS))]
    return _a_out_proj([o for o, _ in outs], [l for _, l in outs], expand, unperms, w_out.astype(BF16), h)


def _mixer_b(h, norm_gain, w_in, w_gate_f, bias_f, w_gate_b, bias_b, out_gain, w_out, *, batch, seq):
    wq, wk, wv, wvt, wr, wz, wgf, wgb = _b_layer_params(w_in, w_gate_f, w_gate_b)
    q, k, v, vt, silu_r, la_f, la_b = _b_in_proj(h, norm_gain, wq, wk, wv, wvt, wr, wz, wgf, wgb,
                                                bias_f[None, :], bias_b[None, :])
    o_f = _gla_fwd(q, k, v, vt, la_f, _tri(False), batch=batch, seq=seq)
    return _gla_bwd_out(q, k, v, vt, la_b, _tri(True), o_f, silu_r, out_gain.reshape(1, B_V_WIDTH),
                        w_out.astype(BF16), h, batch=batch, seq=seq)


def kernel(x, attn_norm, ffn_norm, a_w_in, a_q_norm, a_k_norm, a_w_out, b_w_in, b_w_gate_f, b_gate_bias_f,
           b_w_gate_b, b_gate_bias_b, b_out_norm, b_w_out, ffn_w_gate_up, ffn_w_down):
    batch, seq, _ = x.shape
    h = x.reshape(batch * seq, D_MODEL)
    tables = _rope_tables(seq) + (_segment_ones(), _head_expand()) + _phase_perms()
    for i in range(DEPTH):
        j = i // 2
        if i % 2 == 0:
            h = _mixer_a(h, attn_norm[i][None, :], a_w_in[j], a_q_norm[j], a_k_norm[j], a_w_out[j], tables,
                         batch=batch, seq=seq)
        else:
            h = _mixer_b(h, attn_norm[i][None, :], b_w_in[j], b_w_gate_f[j], b_gate_bias_f[j], b_w_gate_b[j],
                         b_gate_bias_b[j], b_out_norm[j], b_w_out[j], batch=batch, seq=seq)
        wgu, wd = _ffn_params(ffn_w_gate_up[i], ffn_w_down[i])
        h = _ffn(h, ffn_norm[i][None, :], wgu, wd)
    return h.reshape(batch, seq, D_MODEL)
```

```python
import functools

import numpy as np
import jax
import jax.numpy as jnp
from jax import lax
from jax.experimental import pallas as pl
from jax.experimental.pallas import tpu as pltpu

F32 = jnp.float32
BF16 = jnp.bfloat16

D_MODEL = 1024
DEPTH = 4
RMS_EPS = 1e-6
NEG_INF = -1e30

A_GROUPS = ((128, 1), (512, 4), (2048, 16))
A_HEADS = 16
A_HEAD_DIM = 64
A_HALF = 64
ROPE_THETA = 10000.0

B_HEADS = 4
B_KEY_DIM = 128
B_VAL_DIM = 256
B_QK_WIDTH = B_HEADS * B_KEY_DIM
B_V_WIDTH = B_HEADS * B_VAL_DIM
B_GATE_RANK = 16
B_GATE_TAU = 16.0
B_CHUNK = 64

FFN_HIDDEN = 2816

LANES = 128
V7X_VMEM_BYTES = 64 * 1024 * 1024
VMEM_LIMIT_BYTES = V7X_VMEM_BYTES - 8 * 1024 * 1024

Q_TILE = 128
K_TILE = Q_TILE + 2 * A_HALF
HEADS_PER_SLAB = 4
SLAB = HEADS_PER_SLAB * A_HEAD_DIM
PHASE_BLOCK = 256

for _w, _d in A_GROUPS:
    assert _w // (2 * _d) == A_HALF


def _cparams(sem):
    return pltpu.CompilerParams(dimension_semantics=sem, vmem_limit_bytes=VMEM_LIMIT_BYTES)


def _nt_dot(a, b):
    return lax.dot_general(a, b, (((1,), (1,)), ((), ())), preferred_element_type=F32)


def _dot(a, b):
    return jnp.dot(a, b, preferred_element_type=F32)


def _rms_rows(x, gain):
    ms = jnp.mean(x * x, axis=-1, keepdims=True)
    return x * lax.rsqrt(ms + RMS_EPS) * gain


def _a_in_kernel(h_ref, ng_ref, w_ref, qkg_ref, cos_ref, sin_ref, seg_ref, perm_ref, out_ref, xn_ref, xnat_ref,
                 *, tm):
    j = pl.program_id(1)
    sub = PHASE_BLOCK

    @pl.when(j == 0)
    def _():
        xn = _rms_rows(h_ref[...], ng_ref[...]).astype(BF16)
        xn_ref[...] = xn
        xnat_ref[...] = xn

    for g in range(1, len(A_GROUPS)):
        @pl.when(j == 3 * g)
        def _(g=g):
            for s in range(tm // sub):
                rows = slice(s * sub, (s + 1) * sub)
                xn_ref[rows, :] = _dot(perm_ref[g - 1], xnat_ref[rows, :]).astype(BF16)

    kind = j % 3

    @pl.when(kind == 2)
    def _():
        for s in range(tm // sub):
            rows = slice(s * sub, (s + 1) * sub)
            out_ref[rows, :] = _dot(xn_ref[rows, :], w_ref[...]).astype(BF16)

    @pl.when(kind != 2)
    def _():
        scale = jnp.where(kind == 0, A_HEAD_DIM ** -0.5, 1.0).astype(F32)
        for s in range(tm // sub):
            rows = slice(s * sub, (s + 1) * sub)
            acc = _dot(xn_ref[rows, :], w_ref[...])
            cos = cos_ref[rows, :]
            sin = sin_ref[rows, :]
            for pair in range(D_MODEL // (2 * SLAB)):
                ab = []
                for g4 in (2 * pair, 2 * pair + 1):
                    c0 = g4 * SLAB
                    ab.append((acc[:, c0:c0 + LANES], acc[:, c0 + LANES:c0 + SLAB]))
                sq = jnp.concatenate([a * a + b * b for a, b in ab], axis=1).astype(BF16)
                ss = _dot(sq, seg_ref[...])
                for k, (a, b) in enumerate(ab):
                    c0 = (2 * pair + k) * SLAB
                    r = lax.rsqrt(ss[:, k * LANES:(k + 1) * LANES] * (1.0 / A_HEAD_DIM) + RMS_EPS) * scale
                    an = a * r * qkg_ref[:, c0:c0 + LANES]
                    bn = b * r * qkg_ref[:, c0 + LANES:c0 + SLAB]
                    out_ref[rows, c0:c0 + LANES] = (an * cos - bn * sin).astype(BF16)
                    out_ref[rows, c0 + LANES:c0 + SLAB] = (bn * cos + an * sin).astype(BF16)


def _a_in_proj(h, norm_gain, w_perm, qk_gain, cos_t, sin_t, seg, perms, *, seq, tm=1024):
    tokens = h.shape[0]
    n_chunks = w_perm.shape[1] // D_MODEL
    tm = min(tm, seq)
    pos_blocks = seq // tm
    table = pl.BlockSpec((None, tm, LANES), lambda i, j: (j // 3, i % pos_blocks, 0))
    return pl.pallas_call(
        functools.partial(_a_in_kernel, tm=tm),
        grid=(tokens // tm, n_chunks),
        in_specs=[
            pl.BlockSpec((tm, D_MODEL), lambda i, j: (i, 0)),
            pl.BlockSpec((1, D_MODEL), lambda i, j: (0, 0)),
            pl.BlockSpec((D_MODEL, D_MODEL), lambda i, j: (0, j)),
            pl.BlockSpec((1, D_MODEL), lambda i, j: (0, j)),
            table,
            table,
            pl.BlockSpec(seg.shape, lambda i, j: (0, 0)),
            pl.BlockSpec(perms.shape, lambda i, j: (0, 0, 0)),
        ],
        out_specs=pl.BlockSpec((tm, D_MODEL), lambda i, j: (i, j)),
        out_shape=jax.ShapeDtypeStruct((tokens, n_chunks * D_MODEL), BF16),
        scratch_shapes=[pltpu.VMEM((tm, D_MODEL), BF16), pltpu.VMEM((tm, D_MODEL), BF16)],
        compiler_params=_cparams(("parallel", "arbitrary")),
        name="a_in_proj",
    )(h, norm_gain, w_perm, qk_gain, cos_t, sin_t, seg, perms)


def _attn_kernel(q_ref, kp_ref, kc_ref, kn_ref, vp_ref, vc_ref, vn_ref, o_ref, lse_ref,
                 q2_ref, kcat_ref, vcat_ref, o2_ref, lse2_ref, *, tq, phase_len):
    i = pl.program_id(2)
    flat = lambda ref: ref[...].reshape(-1, ref.shape[-1])
    q2_ref[...] = flat(q_ref)
    kcat_ref[0:A_HALF, :] = flat(kp_ref)
    kcat_ref[A_HALF:A_HALF + tq, :] = flat(kc_ref)
    kcat_ref[A_HALF + tq:2 * A_HALF + tq, :] = flat(kn_ref)
    vcat_ref[0:A_HALF, :] = flat(vp_ref)
    vcat_ref[A_HALF:A_HALF + tq, :] = flat(vc_ref)
    vcat_ref[A_HALF + tq:2 * A_HALF + tq, :] = flat(vn_ref)

    stack = HEADS_PER_SLAB * Q_TILE
    row = lax.broadcasted_iota(jnp.int32, (stack, K_TILE), 0) & (Q_TILE - 1)
    col = lax.broadcasted_iota(jnp.int32, (stack, K_TILE), 1)
    band = (col >= row) & (col <= row + 2 * A_HALF)
    col1 = lax.broadcasted_iota(jnp.int32, (Q_TILE, K_TILE), 1)
    q_head = (col1 & (LANES - 1)) >> 5
    q_keep = [(q_head == hi).astype(F32).astype(BF16) for hi in range(HEADS_PER_SLAB)]
    v_head = col1 >> 6
    lse_col = lax.broadcasted_iota(jnp.int32, (Q_TILE, LANES), 1)

    def tile(s, carry):
        r0 = pl.multiple_of(s * Q_TILE, Q_TILE)
        key_pos = i * tq + s * Q_TILE - A_HALF + col
        valid = band & (key_pos >= 0) & (key_pos < phase_len)
        lse_blk = jnp.zeros((Q_TILE, LANES), F32)
        for hg in range(D_MODEL // SLAB):
            cs = slice(hg * SLAB, (hg + 1) * SLAB)
            q4 = q2_ref[pl.ds(r0, Q_TILE), cs]
            k4 = kcat_ref[pl.ds(r0, K_TILE), cs]
            v4 = vcat_ref[pl.ds(r0, K_TILE), cs]
            qm = jnp.concatenate([q4 * q_keep[hi] for hi in range(HEADS_PER_SLAB)], axis=0)
            sc = jnp.where(valid, _nt_dot(qm, k4), NEG_INF)
            m = jnp.max(sc, axis=-1, keepdims=True)
            p = jnp.exp(sc - m)
            l = jnp.sum(p, axis=-1, keepdims=True)
            pv = _dot(p.astype(BF16), v4) * (1.0 / l)
            lse = jnp.broadcast_to(m + jnp.log(l), (stack, LANES))
            o4 = jnp.zeros((Q_TILE, SLAB), F32)
            for hi in range(HEADS_PER_SLAB):
                rows = slice(hi * Q_TILE, (hi + 1) * Q_TILE)
                o4 = jnp.where(v_head == hi, pv[rows], o4)
                lse_blk = jnp.where(lse_col == hg * HEADS_PER_SLAB + hi, lse[rows], lse_blk)
            o2_ref[pl.ds(r0, Q_TILE), cs] = o4.astype(BF16)
        lse2_ref[pl.ds(r0, Q_TILE), :] = lse_blk
        return carry

    lax.fori_loop(0, tq // Q_TILE, tile, 0)
    o_ref[...] = o2_ref[...].reshape(o_ref.shape)
    lse_ref[...] = lse2_ref[...].reshape(lse_ref.shape)


def _attn_group(qkv, group, *, batch, seq, tq=512):
    dil = A_GROUPS[group][1]
    tokens, width = qkv.shape
    phase_len = seq // dil
    tq = min(tq, phase_len)
    nq = phase_len // tq
    hb = phase_len // A_HALF
    r = tq // A_HALF
    run = PHASE_BLOCK // dil
    cq = group * 3
    cur_i = lambda b, i: b * nq + i
    prev_i = lambda b, i: b * hb + jnp.maximum(i * r - 1, 0)
    next_i = lambda b, i: b * hb + jnp.minimum((i + 1) * r, hb - 1)

    if dil == 1:
        view_shape = lambda cols: (tokens, cols)
        spec = lambda rows, cols, row_i, col_i: pl.BlockSpec(
            (rows, cols), lambda b, p, i: (row_i(b, i), col_i))
    else:
        view_shape = lambda cols: (tokens // PHASE_BLOCK, dil, run, cols)
        spec = lambda rows, cols, row_i, col_i: pl.BlockSpec(
            (rows // run, None, run, cols), lambda b, p, i: (row_i(b, i), p, 0, col_i))
    cur = lambda c: spec(tq, D_MODEL, cur_i, c)
    prev = lambda c: spec(A_HALF, D_MODEL, prev_i, c)
    nxt = lambda c: spec(A_HALF, D_MODEL, next_i, c)

    qkv_v = qkv.reshape(view_shape(width))
    o, lse = pl.pallas_call(
        functools.partial(_attn_kernel, tq=tq, phase_len=phase_len),
        grid=(batch, dil, nq),
        in_specs=[cur(cq), prev(cq + 1), cur(cq + 1), nxt(cq + 1), prev(cq + 2), cur(cq + 2), nxt(cq + 2)],
        out_specs=[spec(tq, D_MODEL, cur_i, 0), spec(tq, LANES, cur_i, 0)],
        out_shape=[
            jax.ShapeDtypeStruct(view_shape(D_MODEL), BF16),
            jax.ShapeDtypeStruct(view_shape(LANES), F32),
        ],
        scratch_shapes=[pltpu.VMEM((tq, D_MODEL), BF16),
                        pltpu.VMEM((tq + 2 * A_HALF, D_MODEL), BF16),
                        pltpu.VMEM((tq + 2 * A_HALF, D_MODEL), BF16),
                        pltpu.VMEM((tq, D_MODEL), BF16),
                        pltpu.VMEM((tq, LANES), F32)],
        compiler_params=_cparams(("parallel", "parallel", "parallel")),
        name=f"attn_group{group}",
    )(qkv_v, qkv_v, qkv_v, qkv_v, qkv_v, qkv_v, qkv_v)
    return o.reshape(tokens, D_MODEL), lse.reshape(tokens, LANES)


def _split3(x):
    hi = x.astype(BF16)
    r1 = x - hi.astype(F32)
    mid = r1.astype(BF16)
    return hi, mid, (r1 - mid.astype(F32)).astype(BF16)


def _a_out_kernel(o0_ref, o1_ref, o2_ref, l0_ref, l1_ref, l2_ref, exp_ref, unperm_ref, w_ref, h_ref, out_ref,
                  *, tm):
    for s in range(tm // PHASE_BLOCK):
        rows = slice(s * PHASE_BLOCK, (s + 1) * PHASE_BLOCK)
        outs = [o0_ref[rows, :].astype(F32)]
        lses = [l0_ref[rows, :]]
        for g, (o_ref, l_ref) in enumerate(((o1_ref, l1_ref), (o2_ref, l2_ref))):
            outs.append(_dot(unperm_ref[g], o_ref[rows, :]))
            lses.append(sum(_dot(unperm_ref[g], t) for t in _split3(l_ref[rows, :])))
        m = jnp.maximum(jnp.maximum(lses[0], lses[1]), lses[2])
        es = [jnp.exp(x - m) for x in lses]
        inv = 1.0 / (es[0] + es[1] + es[2])
        mix = None
        for e, o in zip(es, outs):
            alpha = e * inv
            hi = alpha.astype(BF16)
            lo = (alpha - hi.astype(F32)).astype(BF16)
            wide = _dot(jnp.concatenate([hi, lo], axis=1), exp_ref[...])
            mix = wide * o if mix is None else mix + wide * o
        out_ref[rows, :] = h_ref[rows, :] + _dot(mix.astype(BF16), w_ref[...])


def _a_out_proj(os_, lses, expand, unperms, w_out, h, *, tm=512):
    tokens = h.shape[0]
    row = lambda i: (i, 0)
    const = lambda i: (0, 0)
    return pl.pallas_call(
        functools.partial(_a_out_kernel, tm=tm),
        grid=(tokens // tm,),
        in_specs=[pl.BlockSpec((tm, D_MODEL), row)] * 3 + [pl.BlockSpec((tm, LANES), row)] * 3 + [
            pl.BlockSpec((2 * LANES, D_MODEL), const),
            pl.BlockSpec(unperms.shape, lambda i: (0, 0, 0)),
            pl.BlockSpec((D_MODEL, D_MODEL), const),
            pl.BlockSpec((tm, D_MODEL), row),
        ],
        out_specs=pl.BlockSpec((tm, D_MODEL), row),
        out_shape=jax.ShapeDtypeStruct((tokens, D_MODEL), F32),
        compiler_params=_cparams(("parallel",)),
        name="a_out_proj",
    )(*os_, *lses, expand, unperms, w_out, h)


def _ffn_kernel(h_ref, g_ref, wgu_ref, wd_ref, out_ref, *, n_chunks, th):
    x = h_ref[...]
    xn = _rms_rows(x, g_ref[...]).astype(BF16)
    acc = x
    for c in range(n_chunks):
        gu = _dot(xn, wgu_ref[c])
        gate = gu[:, :th]
        up = gu[:, th:]
        act = (gate * jax.nn.sigmoid(gate) * up).astype(BF16)
        acc = acc + _dot(act, wd_ref[c])
    out_ref[...] = acc


def _ffn(h, gain, wgu, wd, *, tm=512):
    tokens = h.shape[0]
    n_chunks, _, th2 = wgu.shape
    th = th2 // 2
    return pl.pallas_call(
        functools.partial(_ffn_kernel, n_chunks=n_chunks, th=th),
        grid=(tokens // tm,),
        in_specs=[
            pl.BlockSpec((tm, D_MODEL), lambda i: (i, 0)),
            pl.BlockSpec((1, D_MODEL), lambda i: (0, 0)),
            pl.BlockSpec((n_chunks, D_MODEL, th2), lambda i: (0, 0, 0)),
            pl.BlockSpec((n_chunks, th, D_MODEL), lambda i: (0, 0, 0)),
        ],
        out_specs=pl.BlockSpec((tm, D_MODEL), lambda i: (i, 0)),
        out_shape=jax.ShapeDtypeStruct((tokens, D_MODEL), F32),
        compiler_params=_cparams(("parallel",)),
        name="ffn",
    )(h, gain, wgu, wd)


def _log_sigmoid(x):
    return jnp.minimum(x, 0.0) - jnp.log1p(jnp.exp(-jnp.abs(x)))


def _b_in_kernel(h_ref, ng_ref, wq_ref, wk_ref, wv_ref, wvt_ref, wr_ref, wz_ref, wgf_ref, wgb_ref,
                 bf_ref, bb_ref, q_ref, k_ref, v_ref, vt_ref, sr_ref, laf_ref, lab_ref):
    xn = _rms_rows(h_ref[...], ng_ref[...]).astype(BF16)
    q_ref[...] = (_dot(xn, wq_ref[...]) * (B_KEY_DIM ** -0.5)).astype(BF16)
    k_ref[...] = _dot(xn, wk_ref[...]).astype(BF16)
    v_ref[...] = _dot(xn, wv_ref[...]).astype(BF16)
    vt_ref[...] = _nt_dot(wvt_ref[...], xn).astype(BF16)
    r = _dot(xn, wr_ref[...])
    sr_ref[...] = (r * jax.nn.sigmoid(r)).astype(BF16)
    z = _dot(xn, wz_ref[...]).astype(BF16)
    laf_ref[...] = _log_sigmoid(_dot(z, wgf_ref[...]) + bf_ref[...]) * (1.0 / B_GATE_TAU)
    lab_ref[...] = _log_sigmoid(_dot(z, wgb_ref[...]) + bb_ref[...]) * (1.0 / B_GATE_TAU)


def _b_in_proj(h, norm_gain, wq, wk, wv, wvt, wr, wz, wgf, wgb, bias_f, bias_b, *, tm=512):
    tokens = h.shape[0]
    row = lambda i: (i, 0)
    const = lambda i: (0, 0)
    full = lambda a: pl.BlockSpec(a.shape, const)
    return pl.pallas_call(
        _b_in_kernel,
        grid=(tokens // tm,),
        in_specs=[pl.BlockSpec((tm, D_MODEL), row)] + [full(a) for a in
                  (norm_gain, wq, wk, wv, wvt, wr, wz, wgf, wgb, bias_f, bias_b)],
        out_specs=[
            pl.BlockSpec((tm, B_QK_WIDTH), row),
            pl.BlockSpec((tm, B_QK_WIDTH), row),
            pl.BlockSpec((tm, B_V_WIDTH), row),
            pl.BlockSpec((B_V_WIDTH, tm), lambda i: (0, i)),
            pl.BlockSpec((tm, B_V_WIDTH), row),
            pl.BlockSpec((tm, B_QK_WIDTH), row),
            pl.BlockSpec((tm, B_QK_WIDTH), row),
        ],
        out_shape=[
            jax.ShapeDtypeStruct((tokens, B_QK_WIDTH), BF16),
            jax.ShapeDtypeStruct((tokens, B_QK_WIDTH), BF16),
            jax.ShapeDtypeStruct((tokens, B_V_WIDTH), BF16),
            jax.ShapeDtypeStruct((B_V_WIDTH, tokens), BF16),
            jax.ShapeDtypeStruct((tokens, B_V_WIDTH), BF16),
            jax.ShapeDtypeStruct((tokens, B_QK_WIDTH), F32),
            jax.ShapeDtypeStruct((tokens, B_QK_WIDTH), F32),
        ],
        compiler_params=_cparams(("parallel",)),
        name="b_in_proj",
    )(h, norm_gain, wq, wk, wv, wvt, wr, wz, wgf, wgb, bias_f, bias_b)


def _gla_block(q_ref, k_ref, v_ref, vt_ref, la_ref, tri_ref, st_ref, o_ref, scr, *, cb, reverse):
    qt_ref, kt_ref, ke_ref, dec_ref, kv_ref, sin_ref = scr
    n_ch = cb // B_CHUNK
    pair_w = 2 * B_CHUNK
    row = lax.broadcasted_iota(jnp.int32, (B_CHUNK, B_CHUNK), 0)
    col = lax.broadcasted_iota(jnp.int32, (B_CHUNK, B_CHUNK), 1)
    amask = (col > row) if reverse else (col <= row)
    heads = [(slice(hh * B_KEY_DIM, (hh + 1) * B_KEY_DIM), slice(hh * B_VAL_DIM, (hh + 1) * B_VAL_DIM))
             for hh in range(B_HEADS)]
    chunk_rows = lambda c: slice(c * B_CHUNK, (c + 1) * B_CHUNK)

    for c in range(n_ch):
        rows = chunk_rows(c)
        g = la_ref[rows, :]
        g_hi = g.astype(BF16)
        g_lo = (g - g_hi.astype(F32)).astype(BF16)
        b = _dot(tri_ref[...], g_hi) + _dot(tri_ref[...], g_lo)
        dec = jnp.exp(b[0:1, :] if reverse else b[B_CHUNK - 1:B_CHUNK, :])
        dec_ref[c:c + 1, :] = dec
        qt_ref[rows, :] = (q_ref[rows, :].astype(F32) * jnp.exp(b)).astype(BF16)
        kt_f = k_ref[rows, :].astype(F32) * jnp.exp(-b)
        kt_ref[rows, :] = kt_f.astype(BF16)
        ke_ref[rows, :] = (kt_f * dec).astype(BF16)

    for c in range(n_ch):
        rows = chunk_rows(c)
        for sk, sv in heads:
            att = jnp.where(amask, _nt_dot(qt_ref[rows, sk], kt_ref[rows, sk]), 0.0)
            o_ref[rows, sv] = _dot(att.astype(BF16), v_ref[rows, sv])

    zero = jnp.zeros((B_CHUNK, B_KEY_DIM), BF16)
    for pair in range(n_ch // 2):
        r0, r1 = chunk_rows(2 * pair), chunk_rows(2 * pair + 1)
        for hh, (sk, sv) in enumerate(heads):
            rhs = jnp.concatenate([jnp.concatenate([ke_ref[r0, sk], zero], axis=1),
                                   jnp.concatenate([zero, ke_ref[r1, sk]], axis=1)], axis=0)
            kv2 = _dot(vt_ref[sv, pair * pair_w:(pair + 1) * pair_w], rhs)
            kv_ref[2 * pair, hh] = kv2[:, :B_KEY_DIM]
            kv_ref[2 * pair + 1, hh] = kv2[:, B_KEY_DIM:]

    for step in range(n_ch):
        c = n_ch - 1 - step if reverse else step
        for hh, (sk, _) in enumerate(heads):
            st = st_ref[hh]
            sin_ref[c, hh] = st.astype(BF16)
            st_ref[hh] = st * dec_ref[c:c + 1, sk] + kv_ref[c, hh]

    for c in range(n_ch):
        rows = chunk_rows(c)
        for hh, (sk, sv) in enumerate(heads):
            o_ref[rows, sv] = o_ref[rows, sv] + _nt_dot(qt_ref[rows, sk], sin_ref[c, hh])


def _gla_scratch(cb):
    n_ch = cb // B_CHUNK
    return [pltpu.VMEM((cb, B_QK_WIDTH), BF16), pltpu.VMEM((cb, B_QK_WIDTH), BF16),
            pltpu.VMEM((cb, B_QK_WIDTH), BF16), pltpu.VMEM((n_ch, B_QK_WIDTH), F32),
            pltpu.VMEM((n_ch, B_HEADS, B_VAL_DIM, B_KEY_DIM), F32),
            pltpu.VMEM((n_ch, B_HEADS, B_VAL_DIM, B_KEY_DIM), BF16)]


def _gla_fwd_kernel(q_ref, k_ref, v_ref, vt_ref, la_ref, tri_ref, o_ref, st_ref, *scr, cb):
    @pl.when(pl.program_id(1) == 0)
    def _():
        st_ref[...] = jnp.zeros_like(st_ref)

    _gla_block(q_ref, k_ref, v_ref, vt_ref, la_ref, tri_ref, st_ref, o_ref, scr, cb=cb, reverse=False)


def _gla_bwd_out_kernel(q_ref, k_ref, v_ref, vt_ref, la_ref, tri_ref, of_ref, sr_ref, og_ref, w_ref,
                        h_ref, out_ref, st_ref, ob_ref, *scr, cb):
    @pl.when(pl.program_id(1) == 0)
    def _():
        st_ref[...] = jnp.zeros_like(st_ref)

    _gla_block(q_ref, k_ref, v_ref, vt_ref, la_ref, tri_ref, st_ref, ob_ref, scr, cb=cb, reverse=True)
    ys = []
    for hh in range(B_HEADS):
        sv = slice(hh * B_VAL_DIM, (hh + 1) * B_VAL_DIM)
        o = of_ref[:, sv] + ob_ref[:, sv]
        ys.append((_rms_rows(o, og_ref[:, sv]) * sr_ref[:, sv].astype(F32)).astype(BF16))
    out_ref[...] = h_ref[...] + _dot(jnp.concatenate(ys, axis=1), w_ref[...])


def _gla_specs(batch, seq, cb, reverse):
    nb = seq // cb
    blk = (lambda b, i: b * nb + (nb - 1 - i)) if reverse else (lambda b, i: b * nb + i)
    row = lambda width: pl.BlockSpec((cb, width), lambda b, i: (blk(b, i), 0))
    const = lambda shape: pl.BlockSpec(shape, lambda b, i: (0,) * len(shape))
    vt = pl.BlockSpec((B_V_WIDTH, cb), lambda b, i: (0, blk(b, i)))
    return nb, row, const, vt


def _gla_fwd(q, k, v, vt, la, tri, *, batch, seq, cb=512):
    cb = min(cb, seq)
    nb, row, const, vt_spec = _gla_specs(batch, seq, cb, False)
    return pl.pallas_call(
        functools.partial(_gla_fwd_kernel, cb=cb),
        grid=(batch, nb),
        in_specs=[row(B_QK_WIDTH), row(B_QK_WIDTH), row(B_V_WIDTH), vt_spec, row(B_QK_WIDTH),
                  const((B_CHUNK, B_CHUNK))],
        out_specs=row(B_V_WIDTH),
        out_shape=jax.ShapeDtypeStruct((batch * seq, B_V_WIDTH), F32),
        scratch_shapes=[pltpu.VMEM((B_HEADS, B_VAL_DIM, B_KEY_DIM), F32)] + _gla_scratch(cb),
        compiler_params=_cparams(("parallel", "arbitrary")),
        name="gla_fwd",
    )(q, k, v, vt, la, tri)


def _gla_bwd_out(q, k, v, vt, la, tri, o_f, silu_r, out_gain, w_out, h, *, batch, seq, cb=512):
    cb = min(cb, seq)
    nb, row, const, vt_spec = _gla_specs(batch, seq, cb, True)
    return pl.pallas_call(
        functools.partial(_gla_bwd_out_kernel, cb=cb),
        grid=(batch, nb),
        in_specs=[row(B_QK_WIDTH), row(B_QK_WIDTH), row(B_V_WIDTH), vt_spec, row(B_QK_WIDTH),
                  const((B_CHUNK, B_CHUNK)), row(B_V_WIDTH), row(B_V_WIDTH), const((1, B_V_WIDTH)),
                  const((B_V_WIDTH, D_MODEL)), row(D_MODEL)],
        out_specs=row(D_MODEL),
        out_shape=jax.ShapeDtypeStruct((batch * seq, D_MODEL), F32),
        scratch_shapes=[pltpu.VMEM((B_HEADS, B_VAL_DIM, B_KEY_DIM), F32),
                        pltpu.VMEM((cb, B_V_WIDTH), F32)] + _gla_scratch(cb),
        compiler_params=_cparams(("parallel", "arbitrary")),
        name="gla_bwd_out",
    )(q, k, v, vt, la, tri, o_f, silu_r, out_gain, w_out, h)


def _a_layer_params(w_in, q_gain, k_gain):
    n_g, n_slab, half = len(A_GROUPS), D_MODEL // SLAB, A_HEAD_DIM // 2
    w7 = w_in.reshape(D_MODEL, n_g, 3, n_slab, HEADS_PER_SLAB, 2, half)
    w_qk = w7[:, :, :2].transpose(0, 1, 2, 3, 5, 4, 6).reshape(D_MODEL, n_g, 2, D_MODEL)
    w_v = w7[:, :, 2:].reshape(D_MODEL, n_g, 1, D_MODEL)
    w_perm = jnp.concatenate([w_qk, w_v], axis=2).reshape(D_MODEL, n_g * 3 * D_MODEL).astype(BF16)

    def spread(gain):
        g5 = jnp.broadcast_to(gain.reshape(n_g, 1, 2, 1, half), (n_g, n_slab, 2, HEADS_PER_SLAB, half))
        return g5.reshape(n_g, 1, D_MODEL)

    gains = jnp.concatenate([spread(q_gain), spread(k_gain), jnp.ones((n_g, 1, D_MODEL), F32)], axis=1)
    return w_perm, gains.reshape(1, n_g * 3 * D_MODEL).astype(F32)


def _phase_major_index(dil):
    r = np.arange(PHASE_BLOCK)
    run = PHASE_BLOCK // dil
    return (r % run) * dil + r // run


def _rope_tables(seq):
    half = A_HEAD_DIM // 2
    inv_freq = ROPE_THETA ** (-jnp.arange(half, dtype=F32) / half)
    blocks = np.arange(seq // PHASE_BLOCK)[:, None] * PHASE_BLOCK
    pos = np.stack([(blocks + _phase_major_index(d)[None, :]).reshape(seq) for _, d in A_GROUPS])
    ang = jnp.asarray(pos, F32)[:, :, None] * inv_freq[None, None, :]
    reps = LANES // half
    return jnp.tile(jnp.cos(ang), (1, 1, reps)), jnp.tile(jnp.sin(ang), (1, 1, reps))


def _phase_perms():
    fwd = np.stack([np.eye(PHASE_BLOCK)[_phase_major_index(d)] for _, d in A_GROUPS[1:]])
    return jnp.asarray(fwd, BF16), jnp.asarray(fwd.transpose(0, 2, 1), BF16)


def _segment_ones():
    a = np.arange(2 * LANES)
    return jnp.asarray((a[:, None] // 32) == (a[None, :] // 32), BF16)


def _head_expand():
    a = np.arange(2 * LANES) % LANES
    n = np.arange(D_MODEL) // A_HEAD_DIM
    return jnp.asarray(a[:, None] == n[None, :], BF16)


def _tri(reverse):
    r = np.arange(B_CHUNK)
    m = (r[None, :] >= r[:, None]) if reverse else (r[None, :] <= r[:, None])
    return jnp.asarray(m, BF16)


def _ffn_params(w_gate_up, w_down, th=256):
    n_chunks = FFN_HIDDEN // th
    wg = w_gate_up[:, :FFN_HIDDEN].reshape(D_MODEL, n_chunks, th)
    wu = w_gate_up[:, FFN_HIDDEN:].reshape(D_MODEL, n_chunks, th)
    wgu = jnp.concatenate([wg, wu], axis=2).transpose(1, 0, 2).astype(BF16)
    return wgu, w_down.reshape(n_chunks, th, D_MODEL).astype(BF16)


def _b_layer_params(w_in, w_gate_f, w_gate_b):
    cuts = np.cumsum([B_QK_WIDTH, B_QK_WIDTH, B_V_WIDTH, B_V_WIDTH]).tolist()
    wq, wk, wv, wr, wz = jnp.split(w_in, cuts, axis=1)
    wz = jnp.pad(wz, ((0, 0), (0, LANES - 2 * B_GATE_RANK)))
    wgf = jnp.pad(w_gate_f, ((0, LANES - B_GATE_RANK), (0, 0)))
    wgb = jnp.pad(w_gate_b, ((B_GATE_RANK, LANES - 2 * B_GATE_RANK), (0, 0)))
    cast = lambda a: a.astype(BF16)
    return cast(wq), cast(wk), cast(wv), cast(wv.T), cast(wr), cast(wz), cast(wgf), cast(wgb)


def _mixer_a(h, norm_gain, w_in, q_gain, k_gain, w_out, tables, *, batch, seq):
    cos_t, sin_t, seg, expand, perms, unperms = tables
    w_perm, qk_gain = _a_layer_params(w_in, q_gain, k_gain)
    qkv = _a_in_proj(h, norm_gain, w_perm, qk_gain, cos_t, sin_t, seg, perms, seq=seq)
    outs = [_attn_group(qkv, g, batch=batch, seq=seq) for g in range(len(A_GROUPS))]
    return _a_out_proj([o for o, _ in outs], [l for _, l in outs], expand, unperms, w_out.astype(BF16), h)


def _mixer_b(h, norm_gain, w_in, w_gate_f, bias_f, w_gate_b, bias_b, out_gain, w_out, *, batch, seq):
    wq, wk, wv, wvt, wr, wz, wgf, wgb = _b_layer_params(w_in, w_gate_f, w_gate_b)
    q, k, v, vt, silu_r, la_f, la_b = _b_in_proj(h, norm_gain, wq, wk, wv, wvt, wr, wz, wgf, wgb,
                                                bias_f[None, :], bias_b[None, :])
    o_f = _gla_fwd(q, k, v, vt, la_f, _tri(False), batch=batch, seq=seq)
    return _gla_bwd_out(q, k, v, vt, la_b, _tri(True), o_f, silu_r, out_gain.reshape(1, B_V_WIDTH),
                        w_out.astype(BF16), h, batch=batch, seq=seq)


def kernel(x, attn_norm, ffn_norm, a_w_in, a_q_norm, a_k_norm, a_w_out, b_w_in, b_w_gate_f, b_gate_bias_f,
           b_w_gate_b, b_gate_bias_b, b_out_norm, b_w_out, ffn_w_gate_up, ffn_w_down):
    batch, seq, _ = x.shape
    h = x.reshape(batch * seq, D_MODEL)
    tables = _rope_tables(seq) + (_segment_ones(), _head_expand()) + _phase_perms()
    for i in range(DEPTH):
        j = i // 2
        if i % 2 == 0:
            h = _mixer_a(h, attn_norm[i][None, :], a_w_in[j], a_q_norm[j], a_k_norm[j], a_w_out[j], tables,
                         batch=batch, seq=seq)
        else:
            h = _mixer_b(h, attn_norm[i][None, :], b_w_in[j], b_w_gate_f[j], b_gate_bias_f[j], b_w_gate_b[j],
                         b_gate_bias_b[j], b_out_norm[j], b_w_out[j], batch=batch, seq=seq)
        wgu, wd = _ffn_params(ffn_w_gate_up[i], ffn_w_down[i])
        h = _ffn(h, ffn_norm[i][None, :], wgu, wd)
    return h.reshape(batch, seq, D_MODEL)
```

```python
import functools

import numpy as np
import jax
import jax.numpy as jnp
from jax import lax
from jax.experimental import pallas as pl
from jax.experimental.pallas import tpu as pltpu

F32 = jnp.float32
BF16 = jnp.bfloat16

D_MODEL = 1024
DEPTH = 4
RMS_EPS = 1e-6
NEG_INF = -1e30

A_GROUPS = ((128, 1), (512, 4), (2048, 16))
A_HEADS = 16
A_HEAD_DIM = 64
A_HALF = 64
ROPE_THETA = 10000.0
LOG2E = 1.4426950408889634
LN2 = 0.6931471805599453
Q_SCALE = A_HEAD_DIM ** -0.5 * LOG2E

B_HEADS = 4
B_KEY_DIM = 128
B_VAL_DIM = 256
B_QK_WIDTH = B_HEADS * B_KEY_DIM
B_V_WIDTH = B_HEADS * B_VAL_DIM
B_GATE_RANK = 16
B_GATE_TAU = 16.0
B_CHUNK = 64

FFN_HIDDEN = 2816

LANES = 128
V7X_VMEM_BYTES = 64 * 1024 * 1024
VMEM_LIMIT_BYTES = V7X_VMEM_BYTES - 8 * 1024 * 1024

Q_TILE = 128
K_TILE = Q_TILE + 2 * A_HALF
HEADS_PER_SLAB = 4
SLAB = HEADS_PER_SLAB * A_HEAD_DIM
PHASE_BLOCK = 256

for _w, _d in A_GROUPS:
    assert _w // (2 * _d) == A_HALF


def _cparams(sem):
    return pltpu.CompilerParams(dimension_semantics=sem, vmem_limit_bytes=VMEM_LIMIT_BYTES)


def _nt_dot(a, b):
    return lax.dot_general(a, b, (((1,), (1,)), ((), ())), preferred_element_type=F32)


def _dot(a, b):
    return jnp.dot(a, b, preferred_element_type=F32)


def _rms_rows(x, gain):
    ms = jnp.mean(x * x, axis=-1, keepdims=True)
    return x * lax.rsqrt(ms + RMS_EPS) * gain


def _a_in_kernel(h_ref, ng_ref, w_ref, qkg_ref, cos_ref, sin_ref, seg_ref, perm_ref, out_ref, xn_ref, xnat_ref,
                 *, tm):
    j = pl.program_id(1)
    sub = PHASE_BLOCK

    @pl.when(j == 0)
    def _():
        xn = _rms_rows(h_ref[...], ng_ref[...]).astype(BF16)
        xn_ref[...] = xn
        xnat_ref[...] = xn

    for g in range(1, len(A_GROUPS)):
        @pl.when(j == 3 * g)
        def _(g=g):
            for s in range(tm // sub):
                rows = slice(s * sub, (s + 1) * sub)
                xn_ref[rows, :] = _dot(perm_ref[g - 1], xnat_ref[rows, :]).astype(BF16)

    kind = j % 3

    @pl.when(kind == 2)
    def _():
        for s in range(tm // sub):
            rows = slice(s * sub, (s + 1) * sub)
            out_ref[rows, :] = _dot(xn_ref[rows, :], w_ref[j]).astype(BF16)

    @pl.when(kind != 2)
    def _():
        scale = jnp.where(kind == 0, Q_SCALE, 1.0).astype(F32)
        for s in range(tm // sub):
            rows = slice(s * sub, (s + 1) * sub)
            acc = _dot(xn_ref[rows, :], w_ref[j])
            cos = cos_ref[rows, :]
            sin = sin_ref[rows, :]
            for pair in range(D_MODEL // (2 * SLAB)):
                ab = []
                for g4 in (2 * pair, 2 * pair + 1):
                    c0 = g4 * SLAB
                    ab.append((acc[:, c0:c0 + LANES], acc[:, c0 + LANES:c0 + SLAB]))
                sq = jnp.concatenate([a * a + b * b for a, b in ab], axis=1).astype(BF16)
                ss = _dot(sq, seg_ref[...])
                for k, (a, b) in enumerate(ab):
                    c0 = (2 * pair + k) * SLAB
                    r = lax.rsqrt(ss[:, k * LANES:(k + 1) * LANES] * (1.0 / A_HEAD_DIM) + RMS_EPS) * scale
                    an = a * r * qkg_ref[:, c0:c0 + LANES]
                    bn = b * r * qkg_ref[:, c0 + LANES:c0 + SLAB]
                    out_ref[rows, c0:c0 + LANES] = (an * cos - bn * sin).astype(BF16)
                    out_ref[rows, c0 + LANES:c0 + SLAB] = (bn * cos + an * sin).astype(BF16)


def _a_in_proj(h, norm_gain, w_perm, qk_gain, cos_t, sin_t, seg, perms, *, seq, tm=1024):
    tokens = h.shape[0]
    n_chunks = w_perm.shape[0]
    tm = min(tm, seq)
    pos_blocks = seq // tm
    table = pl.BlockSpec((None, tm, LANES), lambda i, j: (j // 3, i % pos_blocks, 0))
    return pl.pallas_call(
        functools.partial(_a_in_kernel, tm=tm),
        grid=(tokens // tm, n_chunks),
        in_specs=[
            pl.BlockSpec((tm, D_MODEL), lambda i, j: (i, 0)),
            pl.BlockSpec((1, D_MODEL), lambda i, j: (0, 0)),
            pl.BlockSpec(w_perm.shape, lambda i, j: (0, 0, 0), pipeline_mode=pl.Buffered(1)),
            pl.BlockSpec((1, D_MODEL), lambda i, j: (0, j)),
            table,
            table,
            pl.BlockSpec(seg.shape, lambda i, j: (0, 0)),
            pl.BlockSpec(perms.shape, lambda i, j: (0, 0, 0)),
        ],
        out_specs=pl.BlockSpec((tm, D_MODEL), lambda i, j: (i, j)),
        out_shape=jax.ShapeDtypeStruct((tokens, n_chunks * D_MODEL), BF16),
        scratch_shapes=[pltpu.VMEM((tm, D_MODEL), BF16), pltpu.VMEM((tm, D_MODEL), BF16)],
        compiler_params=_cparams(("parallel", "arbitrary")),
        name="a_in_proj",
    )(h, norm_gain, w_perm, qk_gain, cos_t, sin_t, seg, perms)


def _attn_kernel(q_ref, kp_ref, kc_ref, kn_ref, vp_ref, vc_ref, vn_ref, band_ref, o_ref, lse_ref,
                 q2_ref, kcat_ref, vcat_ref, o2_ref, lse2_ref, *, tq, phase_len):
    i = pl.program_id(2)
    flat = lambda ref: ref[...].reshape(-1, ref.shape[-1])
    q2_ref[...] = flat(q_ref)
    kcat_ref[0:A_HALF, :] = flat(kp_ref)
    kcat_ref[A_HALF:A_HALF + tq, :] = flat(kc_ref)
    kcat_ref[A_HALF + tq:2 * A_HALF + tq, :] = flat(kn_ref)
    vcat_ref[0:A_HALF, :] = flat(vp_ref)
    vcat_ref[A_HALF:A_HALF + tq, :] = flat(vc_ref)
    vcat_ref[A_HALF + tq:2 * A_HALF + tq, :] = flat(vn_ref)

    stack = HEADS_PER_SLAB * Q_TILE
    key_col = lax.broadcasted_iota(jnp.int32, (1, K_TILE), 1)
    col1 = lax.broadcasted_iota(jnp.int32, (Q_TILE, K_TILE), 1)
    q_head = (col1 & (LANES - 1)) >> 5
    q_keep = [(q_head == hi).astype(F32).astype(BF16) for hi in range(HEADS_PER_SLAB)]
    v_head = col1 >> 6
    lse_col = lax.broadcasted_iota(jnp.int32, (Q_TILE, LANES), 1)

    def tile(s):
        r0 = s * Q_TILE
        key_pos = i * tq + s * Q_TILE - A_HALF + key_col
        in_seq = (key_pos >= 0) & (key_pos < phase_len)
        bias = band_ref[...] + jnp.where(in_seq, 0.0, NEG_INF)
        lse_blk = jnp.zeros((Q_TILE, LANES), F32)
        for hg in range(D_MODEL // SLAB):
            cs = slice(hg * SLAB, (hg + 1) * SLAB)
            q4 = q2_ref[pl.ds(r0, Q_TILE), cs]
            k4 = kcat_ref[pl.ds(r0, K_TILE), cs]
            v4 = vcat_ref[pl.ds(r0, K_TILE), cs]
            qm = jnp.concatenate([q4 * q_keep[hi] for hi in range(HEADS_PER_SLAB)], axis=0)
            sc = _nt_dot(qm, k4) + bias
            m = jnp.max(sc, axis=-1, keepdims=True)
            p = jnp.exp2(sc - m)
            l = jnp.sum(p, axis=-1, keepdims=True)
            pv = _dot(p.astype(BF16), v4) * (1.0 / l)
            lse = jnp.broadcast_to((m + jnp.log2(l)) * LN2, (stack, LANES))
            o4 = jnp.zeros((Q_TILE, SLAB), F32)
            for hi in range(HEADS_PER_SLAB):
                rows = slice(hi * Q_TILE, (hi + 1) * Q_TILE)
                o4 = jnp.where(v_head == hi, pv[rows], o4)
                lse_blk = jnp.where(lse_col == hg * HEADS_PER_SLAB + hi, lse[rows], lse_blk)
            o2_ref[pl.ds(r0, Q_TILE), cs] = o4.astype(BF16)
        lse2_ref[pl.ds(r0, Q_TILE), :] = lse_blk

    for s in range(tq // Q_TILE):
        tile(s)
    o_ref[...] = o2_ref[...].reshape(o_ref.shape)
    lse_ref[...] = lse2_ref[...].reshape(lse_ref.shape)


def _attn_group(qkv, group, band, *, batch, seq, tq=512):
    dil = A_GROUPS[group][1]
    tokens, width = qkv.shape
    phase_len = seq // dil
    tq = min(tq, phase_len)
    nq = phase_len // tq
    hb = phase_len // A_HALF
    r = tq // A_HALF
    run = PHASE_BLOCK // dil
    cq = group * 3
    cur_i = lambda b, i: b * nq + i
    prev_i = lambda b, i: b * hb + jnp.maximum(i * r - 1, 0)
    next_i = lambda b, i: b * hb + jnp.minimum((i + 1) * r, hb - 1)

    if dil == 1:
        view_shape = lambda cols: (tokens, cols)
        spec = lambda rows, cols, row_i, col_i: pl.BlockSpec(
            (rows, cols), lambda b, p, i: (row_i(b, i), col_i))
    else:
        view_shape = lambda cols: (tokens // PHASE_BLOCK, dil, run, cols)
        spec = lambda rows, cols, row_i, col_i: pl.BlockSpec(
            (rows // run, None, run, cols), lambda b, p, i: (row_i(b, i), p, 0, col_i))
    cur = lambda c: spec(tq, D_MODEL, cur_i, c)
    prev = lambda c: spec(A_HALF, D_MODEL, prev_i, c)
    nxt = lambda c: spec(A_HALF, D_MODEL, next_i, c)

    qkv_v = qkv.reshape(view_shape(width))
    o, lse = pl.pallas_call(
        functools.partial(_attn_kernel, tq=tq, phase_len=phase_len),
        grid=(batch, dil, nq),
        in_specs=[cur(cq), prev(cq + 1), cur(cq + 1), nxt(cq + 1), prev(cq + 2), cur(cq + 2), nxt(cq + 2),
                  pl.BlockSpec(band.shape, lambda b, p, i: (0, 0))],
        out_specs=[spec(tq, D_MODEL, cur_i, 0), spec(tq, LANES, cur_i, 0)],
        out_shape=[
            jax.ShapeDtypeStruct(view_shape(D_MODEL), BF16),
            jax.ShapeDtypeStruct(view_shape(LANES), F32),
        ],
        scratch_shapes=[pltpu.VMEM((tq, D_MODEL), BF16),
                        pltpu.VMEM((tq + 2 * A_HALF, D_MODEL), BF16),
                        pltpu.VMEM((tq + 2 * A_HALF, D_MODEL), BF16),
                        pltpu.VMEM((tq, D_MODEL), BF16),
                        pltpu.VMEM((tq, LANES), F32)],
        compiler_params=_cparams(("parallel", "parallel", "parallel")),
        name=f"attn_group{group}",
    )(qkv_v, qkv_v, qkv_v, qkv_v, qkv_v, qkv_v, qkv_v, band)
    return o.reshape(tokens, D_MODEL), lse.reshape(tokens, LANES)


def _split3(x):
    hi = x.astype(BF16)
    r1 = x - hi.astype(F32)
    mid = r1.astype(BF16)
    return hi, mid, (r1 - mid.astype(F32)).astype(BF16)


def _a_out_kernel(o0_ref, o1_ref, o2_ref, l0_ref, l1_ref, l2_ref, exp_ref, unperm_ref, w_ref, h_ref, out_ref,
                  *, tm):
    for s in range(tm // PHASE_BLOCK):
        rows = slice(s * PHASE_BLOCK, (s + 1) * PHASE_BLOCK)
        outs = [o0_ref[rows, :].astype(F32)]
        lses = [l0_ref[rows, :]]
        for g, (o_ref, l_ref) in enumerate(((o1_ref, l1_ref), (o2_ref, l2_ref))):
            outs.append(_dot(unperm_ref[g], o_ref[rows, :]))
            lses.append(sum(_dot(unperm_ref[g], t) for t in _split3(l_ref[rows, :])))
        m = jnp.maximum(jnp.maximum(lses[0], lses[1]), lses[2])
        es = [jnp.exp(x - m) for x in lses]
        inv = 1.0 / (es[0] + es[1] + es[2])
        mix = None
        for e, o in zip(es, outs):
            alpha = e * inv
            hi = alpha.astype(BF16)
            lo = (alpha - hi.astype(F32)).astype(BF16)
            wide = _dot(jnp.concatenate([hi, lo], axis=1), exp_ref[...])
            mix = wide * o if mix is None else mix + wide * o
        out_ref[rows, :] = h_ref[rows, :] + _dot(mix.astype(BF16), w_ref[...])


def _a_out_proj(os_, lses, expand, unperms, w_out, h, *, tm=512):
    tokens = h.shape[0]
    row = lambda i: (i, 0)
    const = lambda i: (0, 0)
    return pl.pallas_call(
        functools.partial(_a_out_kernel, tm=tm),
        grid=(tokens // tm,),
        in_specs=[pl.BlockSpec((tm, D_MODEL), row)] * 3 + [pl.BlockSpec((tm, LANES), row)] * 3 + [
            pl.BlockSpec((2 * LANES, D_MODEL), const),
            pl.BlockSpec(unperms.shape, lambda i: (0, 0, 0)),
            pl.BlockSpec((D_MODEL, D_MODEL), const),
            pl.BlockSpec((tm, D_MODEL), row),
        ],
        out_specs=pl.BlockSpec((tm, D_MODEL), row),
        out_shape=jax.ShapeDtypeStruct((tokens, D_MODEL), F32),
        compiler_params=_cparams(("parallel",)),
        name="a_out_proj",
    )(*os_, *lses, expand, unperms, w_out, h)


def _ffn_kernel(h_ref, g_ref, wgu_ref, wd_ref, out_ref, *, th):
    x = h_ref[...]
    xn = _rms_rows(x, g_ref[...]).astype(BF16)
    acc = x
    for c in range(FFN_HIDDEN // th):
        cols = slice(c * th, (c + 1) * th)
        gate = _dot(xn, wgu_ref[:, cols])
        up = _dot(xn, wgu_ref[:, FFN_HIDDEN + c * th:FFN_HIDDEN + (c + 1) * th])
        act = (gate * jax.nn.sigmoid(gate) * up).astype(BF16)
        acc = acc + _dot(act, wd_ref[cols, :])
    out_ref[...] = acc


def _ffn(h, gain, wgu, wd, *, tm=512, th=256):
    tokens = h.shape[0]
    return pl.pallas_call(
        functools.partial(_ffn_kernel, th=th),
        grid=(tokens // tm,),
        in_specs=[
            pl.BlockSpec((tm, D_MODEL), lambda i: (i, 0)),
            pl.BlockSpec((1, D_MODEL), lambda i: (0, 0)),
            pl.BlockSpec(wgu.shape, lambda i: (0, 0)),
            pl.BlockSpec(wd.shape, lambda i: (0, 0)),
        ],
        out_specs=pl.BlockSpec((tm, D_MODEL), lambda i: (i, 0)),
        out_shape=jax.ShapeDtypeStruct((tokens, D_MODEL), F32),
        compiler_params=_cparams(("parallel",)),
        name="ffn",
    )(h, gain, wgu, wd)


def _log_sigmoid(x):
    return jnp.minimum(x, 0.0) - jnp.log1p(jnp.exp(-jnp.abs(x)))


def _b_in_kernel(h_ref, ng_ref, wmain_ref, wvt_ref, wz_ref, wgf_ref, wgb_ref, bf_ref, bb_ref,
                 q_ref, k_ref, v_ref, vt_ref, sr_ref, laf_ref, lab_ref, *, tm):
    sub = PHASE_BLOCK
    for s in range(tm // sub):
        rows = slice(s * sub, (s + 1) * sub)
        xn = _rms_rows(h_ref[rows, :], ng_ref[...]).astype(BF16)
        y = _dot(xn, wmain_ref[...])
        q_ref[rows, :] = (y[:, :B_QK_WIDTH] * (B_KEY_DIM ** -0.5)).astype(BF16)
        k_ref[rows, :] = y[:, B_QK_WIDTH:2 * B_QK_WIDTH].astype(BF16)
        v_ref[rows, :] = y[:, 2 * B_QK_WIDTH:2 * B_QK_WIDTH + B_V_WIDTH].astype(BF16)
        r = y[:, 2 * B_QK_WIDTH + B_V_WIDTH:]
        sr_ref[rows, :] = (r * jax.nn.sigmoid(r)).astype(BF16)
        vt_ref[:, rows] = _nt_dot(wvt_ref[...], xn).astype(BF16)
        z = _dot(xn, wz_ref[...]).astype(BF16)
        laf_ref[rows, :] = _log_sigmoid(_dot(z, wgf_ref[...]) + bf_ref[...]) * (1.0 / B_GATE_TAU)
        lab_ref[rows, :] = _log_sigmoid(_dot(z, wgb_ref[...]) + bb_ref[...]) * (1.0 / B_GATE_TAU)


def _b_in_proj(h, norm_gain, wmain, wvt, wz, wgf, wgb, bias_f, bias_b, *, tm=512):
    tokens = h.shape[0]
    row = lambda i: (i, 0)
    const = lambda i: (0, 0)
    full = lambda a: pl.BlockSpec(a.shape, const)
    return pl.pallas_call(
        functools.partial(_b_in_kernel, tm=tm),
        grid=(tokens // tm,),
        in_specs=[pl.BlockSpec((tm, D_MODEL), row)] + [full(a) for a in
                  (norm_gain, wmain, wvt, wz, wgf, wgb, bias_f, bias_b)],
        out_specs=[
            pl.BlockSpec((tm, B_QK_WIDTH), row),
            pl.BlockSpec((tm, B_QK_WIDTH), row),
            pl.BlockSpec((tm, B_V_WIDTH), row),
            pl.BlockSpec((B_V_WIDTH, tm), lambda i: (0, i)),
            pl.BlockSpec((tm, B_V_WIDTH), row),
            pl.BlockSpec((tm, B_QK_WIDTH), row),
            pl.BlockSpec((tm, B_QK_WIDTH), row),
        ],
        out_shape=[
            jax.ShapeDtypeStruct((tokens, B_QK_WIDTH), BF16),
            jax.ShapeDtypeStruct((tokens, B_QK_WIDTH), BF16),
            jax.ShapeDtypeStruct((tokens, B_V_WIDTH), BF16),
            jax.ShapeDtypeStruct((B_V_WIDTH, tokens), BF16),
            jax.ShapeDtypeStruct((tokens, B_V_WIDTH), BF16),
            jax.ShapeDtypeStruct((tokens, B_QK_WIDTH), F32),
            jax.ShapeDtypeStruct((tokens, B_QK_WIDTH), F32),
        ],
        compiler_params=_cparams(("parallel",)),
        name="b_in_proj",
    )(h, norm_gain, wmain, wvt, wz, wgf, wgb, bias_f, bias_b)


def _gla_block(q_ref, k_ref, v_ref, vt_ref, la_ref, tri_ref, st_ref, o_ref, scr, *, cb, reverse):
    qt_ref, kt_ref, ke_ref, dec_ref, kv_ref, sin_ref = scr
    n_ch = cb // B_CHUNK
    pair_w = 2 * B_CHUNK
    row = lax.broadcasted_iota(jnp.int32, (B_CHUNK, B_CHUNK), 0)
    col = lax.broadcasted_iota(jnp.int32, (B_CHUNK, B_CHUNK), 1)
    amask = (col > row) if reverse else (col <= row)
    heads = [(slice(hh * B_KEY_DIM, (hh + 1) * B_KEY_DIM), slice(hh * B_VAL_DIM, (hh + 1) * B_VAL_DIM))
             for hh in range(B_HEADS)]
    chunk_rows = lambda c: slice(c * B_CHUNK, (c + 1) * B_CHUNK)

    for c in range(n_ch):
        rows = chunk_rows(c)
        g = la_ref[rows, :]
        g_hi = g.astype(BF16)
        g_lo = (g - g_hi.astype(F32)).astype(BF16)
        b = _dot(tri_ref[...], g_hi) + _dot(tri_ref[...], g_lo)
        dec = jnp.exp(b[0:1, :] if reverse else b[B_CHUNK - 1:B_CHUNK, :])
        dec_ref[c:c + 1, :] = dec
        qt_ref[rows, :] = (q_ref[rows, :].astype(F32) * jnp.exp(b)).astype(BF16)
        kt_f = k_ref[rows, :].astype(F32) * jnp.exp(-b)
        kt_ref[rows, :] = kt_f.astype(BF16)
        ke_ref[rows, :] = (kt_f * dec).astype(BF16)

    for c in range(n_ch):
        rows = chunk_rows(c)
        for sk, sv in heads:
            att = jnp.where(amask, _nt_dot(qt_ref[rows, sk], kt_ref[rows, sk]), 0.0)
            o_ref[rows, sv] = _dot(att.astype(BF16), v_ref[rows, sv])

    zero = jnp.zeros((B_CHUNK, B_KEY_DIM), BF16)
    for pair in range(n_ch // 2):
        r0, r1 = chunk_rows(2 * pair), chunk_rows(2 * pair + 1)
        for hh, (sk, sv) in enumerate(heads):
            rhs = jnp.concatenate([jnp.concatenate([ke_ref[r0, sk], zero], axis=1),
                                   jnp.concatenate([zero, ke_ref[r1, sk]], axis=1)], axis=0)
            kv2 = _dot(vt_ref[sv, pair * pair_w:(pair + 1) * pair_w], rhs)
            kv_ref[2 * pair, hh] = kv2[:, :B_KEY_DIM]
            kv_ref[2 * pair + 1, hh] = kv2[:, B_KEY_DIM:]

    for step in range(n_ch):
        c = n_ch - 1 - step if reverse else step
        for hh, (sk, _) in enumerate(heads):
            st = st_ref[hh]
            sin_ref[c, hh] = st.astype(BF16)
            st_ref[hh] = st * dec_ref[c:c + 1, sk] + kv_ref[c, hh]

    for c in range(n_ch):
        rows = chunk_rows(c)
        for hh, (sk, sv) in enumerate(heads):
            o_ref[rows, sv] = o_ref[rows, sv] + _nt_dot(qt_ref[rows, sk], sin_ref[c, hh])


def _gla_scratch(cb):
    n_ch = cb // B_CHUNK
    return [pltpu.VMEM((cb, B_QK_WIDTH), BF16), pltpu.VMEM((cb, B_QK_WIDTH), BF16),
            pltpu.VMEM((cb, B_QK_WIDTH), BF16), pltpu.VMEM((n_ch, B_QK_WIDTH), F32),
            pltpu.VMEM((n_ch, B_HEADS, B_VAL_DIM, B_KEY_DIM), F32),
            pltpu.VMEM((n_ch, B_HEADS, B_VAL_DIM, B_KEY_DIM), BF16)]


def _gla_fwd_kernel(q_ref, k_ref, v_ref, vt_ref, la_ref, tri_ref, o_ref, st_ref, *scr, cb):
    @pl.when(pl.program_id(1) == 0)
    def _():
        st_ref[...] = jnp.zeros_like(st_ref)

    _gla_block(q_ref, k_ref, v_ref, vt_ref, la_ref, tri_ref, st_ref, o_ref, scr, cb=cb, reverse=False)


def _gla_bwd_out_kernel(q_ref, k_ref, v_ref, vt_ref, la_ref, tri_ref, of_ref, sr_ref, og_ref, w_ref,
                        h_ref, out_ref, st_ref, ob_ref, *scr, cb):
    @pl.when(pl.program_id(1) == 0)
    def _():
        st_ref[...] = jnp.zeros_like(st_ref)

    _gla_block(q_ref, k_ref, v_ref, vt_ref, la_ref, tri_ref, st_ref, ob_ref, scr, cb=cb, reverse=True)
    ys = []
    for hh in range(B_HEADS):
        sv = slice(hh * B_VAL_DIM, (hh + 1) * B_VAL_DIM)
        o = of_ref[:, sv] + ob_ref[:, sv]
        ys.append((_rms_rows(o, og_ref[:, sv]) * sr_ref[:, sv].astype(F32)).astype(BF16))
    out_ref[...] = h_ref[...] + _dot(jnp.concatenate(ys, axis=1), w_ref[...])


def _gla_specs(batch, seq, cb, reverse):
    nb = seq // cb
    blk = (lambda b, i: b * nb + (nb - 1 - i)) if reverse else (lambda b, i: b * nb + i)
    row = lambda width: pl.BlockSpec((cb, width), lambda b, i: (blk(b, i), 0))
    const = lambda shape: pl.BlockSpec(shape, lambda b, i: (0,) * len(shape))
    vt = pl.BlockSpec((B_V_WIDTH, cb), lambda b, i: (0, blk(b, i)))
    return nb, row, const, vt


def _gla_fwd(q, k, v, vt, la, tri, *, batch, seq, cb=512):
    cb = min(cb, seq)
    nb, row, const, vt_spec = _gla_specs(batch, seq, cb, False)
    return pl.pallas_call(
        functools.partial(_gla_fwd_kernel, cb=cb),
        grid=(batch, nb),
        in_specs=[row(B_QK_WIDTH), row(B_QK_WIDTH), row(B_V_WIDTH), vt_spec, row(B_QK_WIDTH),
                  const((B_CHUNK, B_CHUNK))],
        out_specs=row(B_V_WIDTH),
        out_shape=jax.ShapeDtypeStruct((batch * seq, B_V_WIDTH), F32),
        scratch_shapes=[pltpu.VMEM((B_HEADS, B_VAL_DIM, B_KEY_DIM), F32)] + _gla_scratch(cb),
        compiler_params=_cparams(("parallel", "arbitrary")),
        name="gla_fwd",
    )(q, k, v, vt, la, tri)


def _gla_bwd_out(q, k, v, vt, la, tri, o_f, silu_r, out_gain, w_out, h, *, batch, seq, cb=512):
    cb = min(cb, seq)
    nb, row, const, vt_spec = _gla_specs(batch, seq, cb, True)
    return pl.pallas_call(
        functools.partial(_gla_bwd_out_kernel, cb=cb),
        grid=(batch, nb),
        in_specs=[row(B_QK_WIDTH), row(B_QK_WIDTH), row(B_V_WIDTH), vt_spec, row(B_QK_WIDTH),
                  const((B_CHUNK, B_CHUNK)), row(B_V_WIDTH), row(B_V_WIDTH), const((1, B_V_WIDTH)),
                  const((B_V_WIDTH, D_MODEL)), row(D_MODEL)],
        out_specs=row(D_MODEL),
        out_shape=jax.ShapeDtypeStruct((batch * seq, D_MODEL), F32),
        scratch_shapes=[pltpu.VMEM((B_HEADS, B_VAL_DIM, B_KEY_DIM), F32),
                        pltpu.VMEM((cb, B_V_WIDTH), F32)] + _gla_scratch(cb),
        compiler_params=_cparams(("parallel", "arbitrary")),
        name="gla_bwd_out",
    )(q, k, v, vt, la, tri, o_f, silu_r, out_gain, w_out, h)


def _a_layer_params(w_in, q_gain, k_gain):
    n_g, n_slab, half = len(A_GROUPS), D_MODEL // SLAB, A_HEAD_DIM // 2
    w7 = w_in.reshape(D_MODEL, n_g, 3, n_slab, HEADS_PER_SLAB, 2, half)
    w_qk = w7[:, :, :2].transpose(0, 1, 2, 3, 5, 4, 6).reshape(D_MODEL, n_g, 2, D_MODEL)
    w_v = w7[:, :, 2:].reshape(D_MODEL, n_g, 1, D_MODEL)
    w_perm = jnp.concatenate([w_qk, w_v], axis=2).astype(BF16)
    w_perm = w_perm.transpose(1, 2, 0, 3).reshape(n_g * 3, D_MODEL, D_MODEL)

    def spread(gain):
        g5 = jnp.broadcast_to(gain.reshape(n_g, 1, 2, 1, half), (n_g, n_slab, 2, HEADS_PER_SLAB, half))
        return g5.reshape(n_g, 1, D_MODEL)

    gains = jnp.concatenate([spread(q_gain), spread(k_gain), jnp.ones((n_g, 1, D_MODEL), F32)], axis=1)
    return w_perm, gains.reshape(1, n_g * 3 * D_MODEL).astype(F32)


def _phase_major_index(dil):
    r = np.arange(PHASE_BLOCK)
    run = PHASE_BLOCK // dil
    return (r % run) * dil + r // run


def _rope_tables(seq):
    half = A_HEAD_DIM // 2
    inv_freq = ROPE_THETA ** (-jnp.arange(half, dtype=F32) / half)
    blocks = np.arange(seq // PHASE_BLOCK)[:, None] * PHASE_BLOCK
    pos = np.stack([(blocks + _phase_major_index(d)[None, :]).reshape(seq) for _, d in A_GROUPS])
    ang = jnp.asarray(pos, F32)[:, :, None] * inv_freq[None, None, :]
    reps = LANES // half
    return jnp.tile(jnp.cos(ang), (1, 1, reps)), jnp.tile(jnp.sin(ang), (1, 1, reps))


def _phase_perms():
    fwd = np.stack([np.eye(PHASE_BLOCK)[_phase_major_index(d)] for _, d in A_GROUPS[1:]])
    return jnp.asarray(fwd, BF16), jnp.asarray(fwd.transpose(0, 2, 1), BF16)


def _band_bias():
    r = np.arange(HEADS_PER_SLAB * Q_TILE)[:, None] % Q_TILE
    c = np.arange(K_TILE)[None, :]
    return jnp.asarray(np.where((c >= r) & (c <= r + 2 * A_HALF), 0.0, NEG_INF), F32)


def _segment_ones():
    a = np.arange(2 * LANES)
    return jnp.asarray((a[:, None] // 32) == (a[None, :] // 32), BF16)


def _head_expand():
    a = np.arange(2 * LANES) % LANES
    n = np.arange(D_MODEL) // A_HEAD_DIM
    return jnp.asarray(a[:, None] == n[None, :], BF16)


def _tri(reverse):
    r = np.arange(B_CHUNK)
    m = (r[None, :] >= r[:, None]) if reverse else (r[None, :] <= r[:, None])
    return jnp.asarray(m, BF16)


def _b_layer_params(w_in, w_gate_f, w_gate_b):
    n_main = 2 * B_QK_WIDTH + 2 * B_V_WIDTH
    wmain = w_in[:, :n_main].astype(BF16)
    wvt = w_in[:, 2 * B_QK_WIDTH:2 * B_QK_WIDTH + B_V_WIDTH].T.astype(BF16)
    wz = jnp.pad(w_in[:, n_main:], ((0, 0), (0, LANES - 2 * B_GATE_RANK))).astype(BF16)
    wgf = jnp.pad(w_gate_f, ((0, LANES - B_GATE_RANK), (0, 0))).astype(BF16)
    wgb = jnp.pad(w_gate_b, ((B_GATE_RANK, LANES - 2 * B_GATE_RANK), (0, 0))).astype(BF16)
    return wmain, wvt, wz, wgf, wgb


def _mixer_a(h, norm_gain, w_in, q_gain, k_gain, w_out, tables, *, batch, seq):
    cos_t, sin_t, seg, expand, perms, unperms, band = tables
    w_perm, qk_gain = _a_layer_params(w_in, q_gain, k_gain)
    qkv = _a_in_proj(h, norm_gain, w_perm, qk_gain, cos_t, sin_t, seg, perms, seq=seq)
    outs = [_attn_group(qkv, g, band, batch=batch, seq=seq) for g in range(len(A_GROUPS))]
    return _a_out_proj([o for o, _ in outs], [l for _, l in outs], expand, unperms, w_out.astype(BF16), h)


def _mixer_b(h, norm_gain, w_in, w_gate_f, bias_f, w_gate_b, bias_b, out_gain, w_out, *, batch, seq):
    wmain, wvt, wz, wgf, wgb = _b_layer_params(w_in, w_gate_f, w_gate_b)
    q, k, v, vt, silu_r, la_f, la_b = _b_in_proj(h, norm_gain, wmain, wvt, wz, wgf, wgb,
                                                bias_f[None, :], bias_b[None, :])
    o_f = _gla_fwd(q, k, v, vt, la_f, _tri(False), batch=batch, seq=seq)
    return _gla_bwd_out(q, k, v, vt, la_b, _tri(True), o_f, silu_r, out_gain.reshape(1, B_V_WIDTH),
                        w_out.astype(BF16), h, batch=batch, seq=seq)


def kernel(x, attn_norm, ffn_norm, a_w_in, a_q_norm, a_k_norm, a_w_out, b_w_in, b_w_gate_f, b_gate_bias_f,
           b_w_gate_b, b_gate_bias_b, b_out_norm, b_w_out, ffn_w_gate_up, ffn_w_down):
    batch, seq, _ = x.shape
    h = x.reshape(batch * seq, D_MODEL)
    tables = _rope_tables(seq) + (_segment_ones(), _head_expand()) + _phase_perms() + (_band_bias(),)
    for i in range(DEPTH):
        j = i // 2
        if i % 2 == 0:
            h = _mixer_a(h, attn_norm[i][None, :], a_w_in[j], a_q_norm[j], a_k_norm[j], a_w_out[j], tables,
                         batch=batch, seq=seq)
        else:
            h = _mixer_b(h, attn_norm[i][None, :], b_w_in[j], b_w_gate_f[j], b_gate_bias_f[j], b_w_gate_b[j],
                         b_gate_bias_b[j], b_out_norm[j], b_w_out[j], batch=batch, seq=seq)
        h = _ffn(h, ffn_norm[i][None, :], ffn_w_gate_up[i].astype(BF16), ffn_w_down[i].astype(BF16))
    return h.reshape(batch, seq, D_MODEL)
```

```python
import functools

import numpy as np
import jax
import jax.numpy as jnp
from jax import lax
from jax.experimental import pallas as pl
from jax.experimental.pallas import tpu as pltpu

F32 = jnp.float32
BF16 = jnp.bfloat16

D_MODEL = 1024
DEPTH = 4
RMS_EPS = 1e-6
NEG_INF = -1e30

A_GROUPS = ((128, 1), (512, 4), (2048, 16))
A_HEADS = 16
A_HEAD_DIM = 64
A_HALF = 64
ROPE_THETA = 10000.0
LOG2E = 1.4426950408889634
LN2 = 0.6931471805599453
Q_SCALE = A_HEAD_DIM ** -0.5 * LOG2E

B_HEADS = 4
B_KEY_DIM = 128
B_VAL_DIM = 256
B_QK_WIDTH = B_HEADS * B_KEY_DIM
B_V_WIDTH = B_HEADS * B_VAL_DIM
B_GATE_RANK = 16
B_GATE_TAU = 16.0
B_CHUNK = 64

FFN_HIDDEN = 2816

LANES = 128
V7X_VMEM_BYTES = 64 * 1024 * 1024
VMEM_LIMIT_BYTES = V7X_VMEM_BYTES - 8 * 1024 * 1024

Q_TILE = 128
K_TILE = Q_TILE + 2 * A_HALF
HEADS_PER_SLAB = 4
SLAB = HEADS_PER_SLAB * A_HEAD_DIM
PHASE_BLOCK = 256

for _w, _d in A_GROUPS:
    assert _w // (2 * _d) == A_HALF


def _cparams(sem):
    return pltpu.CompilerParams(dimension_semantics=sem, vmem_limit_bytes=VMEM_LIMIT_BYTES)


def _nt_dot(a, b):
    return lax.dot_general(a, b, (((1,), (1,)), ((), ())), preferred_element_type=F32)


def _dot(a, b):
    return jnp.dot(a, b, preferred_element_type=F32)


def _rms_rows(x, gain):
    ms = jnp.mean(x * x, axis=-1, keepdims=True)
    return x * lax.rsqrt(ms + RMS_EPS) * gain


def _a_in_kernel(h_ref, ng_ref, w_ref, qkg_ref, cos_ref, sin_ref, seg_ref, perm_ref, out_ref, xn_ref, xnat_ref,
                 *, tm):
    g = pl.program_id(1)
    sub = PHASE_BLOCK

    @pl.when(g == 0)
    def _():
        xn = _rms_rows(h_ref[...], ng_ref[...]).astype(BF16)
        xn_ref[...] = xn
        xnat_ref[...] = xn

    @pl.when(g > 0)
    def _():
        perm = perm_ref[jnp.maximum(g - 1, 0)]
        for s in range(tm // sub):
            rows = slice(s * sub, (s + 1) * sub)
            xn_ref[rows, :] = _dot(perm, xnat_ref[rows, :]).astype(BF16)

    for s in range(tm // sub):
        rows = slice(s * sub, (s + 1) * sub)
        x = xn_ref[rows, :]
        cos = cos_ref[rows, :]
        sin = sin_ref[rows, :]
        for chunk, scale in ((0, Q_SCALE), (1, 1.0)):
            acc = _dot(x, w_ref[3 * g + chunk])
            base = chunk * D_MODEL
            for pair in range(D_MODEL // (2 * SLAB)):
                ab = []
                for g4 in (2 * pair, 2 * pair + 1):
                    c0 = g4 * SLAB
                    ab.append((acc[:, c0:c0 + LANES], acc[:, c0 + LANES:c0 + SLAB]))
                sq = jnp.concatenate([a * a + b * b for a, b in ab], axis=1).astype(BF16)
                ss = _dot(sq, seg_ref[...])
                for k, (a, b) in enumerate(ab):
                    c0 = base + (2 * pair + k) * SLAB
                    r = lax.rsqrt(ss[:, k * LANES:(k + 1) * LANES] * (1.0 / A_HEAD_DIM) + RMS_EPS) * scale
                    an = a * r * qkg_ref[:, c0:c0 + LANES]
                    bn = b * r * qkg_ref[:, c0 + LANES:c0 + SLAB]
                    out_ref[rows, c0:c0 + LANES] = (an * cos - bn * sin).astype(BF16)
                    out_ref[rows, c0 + LANES:c0 + SLAB] = (bn * cos + an * sin).astype(BF16)
        out_ref[rows, 2 * D_MODEL:] = _dot(x, w_ref[3 * g + 2]).astype(BF16)


def _a_in_proj(h, norm_gain, w_perm, qk_gain, cos_t, sin_t, seg, perms, *, seq, tm=1024):
    tokens = h.shape[0]
    n_groups = w_perm.shape[0] // 3
    tm = min(tm, seq)
    pos_blocks = seq // tm
    table = pl.BlockSpec((None, tm, LANES), lambda i, g: (g, i % pos_blocks, 0))
    return pl.pallas_call(
        functools.partial(_a_in_kernel, tm=tm),
        grid=(tokens // tm, n_groups),
        in_specs=[
            pl.BlockSpec((tm, D_MODEL), lambda i, g: (i, 0)),
            pl.BlockSpec((1, D_MODEL), lambda i, g: (0, 0)),
            pl.BlockSpec(w_perm.shape, lambda i, g: (0, 0, 0), pipeline_mode=pl.Buffered(1)),
            pl.BlockSpec((1, 3 * D_MODEL), lambda i, g: (0, g)),
            table,
            table,
            pl.BlockSpec(seg.shape, lambda i, g: (0, 0)),
            pl.BlockSpec(perms.shape, lambda i, g: (0, 0, 0)),
        ],
        out_specs=pl.BlockSpec((tm, 3 * D_MODEL), lambda i, g: (i, g)),
        out_shape=jax.ShapeDtypeStruct((tokens, n_groups * 3 * D_MODEL), BF16),
        scratch_shapes=[pltpu.VMEM((tm, D_MODEL), BF16), pltpu.VMEM((tm, D_MODEL), BF16)],
        compiler_params=_cparams(("parallel", "arbitrary")),
        name="a_in_proj",
    )(h, norm_gain, w_perm, qk_gain, cos_t, sin_t, seg, perms)


def _attn_kernel(q_ref, kp_ref, kc_ref, kn_ref, vp_ref, vc_ref, vn_ref, band_ref, o_ref, st_ref,
                 q2_ref, kcat_ref, vm_ref, o2_ref, st2_ref, *, tq, phase_len):
    i = pl.program_id(2)
    flat = lambda ref: ref[...].reshape(-1, ref.shape[-1])
    q2_ref[...] = flat(q_ref)
    kcat_ref[0:A_HALF, :] = flat(kp_ref)
    kcat_ref[A_HALF:A_HALF + tq, :] = flat(kc_ref)
    kcat_ref[A_HALF + tq:2 * A_HALF + tq, :] = flat(kn_ref)
    lane = lax.broadcasted_iota(jnp.int32, (1, D_MODEL), 1)
    for hi in range(HEADS_PER_SLAB):
        keep = (((lane & (SLAB - 1)) >> 6) == hi).astype(F32).astype(BF16)
        vm_ref[hi, 0:A_HALF, :] = flat(vp_ref) * keep
        vm_ref[hi, A_HALF:A_HALF + tq, :] = flat(vc_ref) * keep
        vm_ref[hi, A_HALF + tq:2 * A_HALF + tq, :] = flat(vn_ref) * keep

    stack = HEADS_PER_SLAB * Q_TILE
    key_col = lax.broadcasted_iota(jnp.int32, (1, K_TILE), 1)
    col1 = lax.broadcasted_iota(jnp.int32, (Q_TILE, K_TILE), 1)
    q_head = (col1 & (LANES - 1)) >> 5
    q_keep = [(q_head == hi).astype(F32).astype(BF16) for hi in range(HEADS_PER_SLAB)]
    st_col = lax.broadcasted_iota(jnp.int32, (Q_TILE, LANES), 1)

    def tile(s):
        r0 = s * Q_TILE
        key_pos = i * tq + s * Q_TILE - A_HALF + key_col
        in_seq = (key_pos >= 0) & (key_pos < phase_len)
        bias = band_ref[...] + jnp.where(in_seq, 0.0, NEG_INF)
        stats = jnp.zeros((Q_TILE, LANES), F32)
        for hg in range(D_MODEL // SLAB):
            cs = slice(hg * SLAB, (hg + 1) * SLAB)
            q4 = q2_ref[pl.ds(r0, Q_TILE), cs]
            k4 = kcat_ref[pl.ds(r0, K_TILE), cs]
            qm = jnp.concatenate([q4 * q_keep[hi] for hi in range(HEADS_PER_SLAB)], axis=0)
            sc = _nt_dot(qm, k4) + bias
            m = jnp.max(sc, axis=-1, keepdims=True)
            p = jnp.exp2(sc - m)
            l = jnp.sum(p, axis=-1, keepdims=True)
            pb = p.astype(BF16)
            m_rep = jnp.broadcast_to(m, (stack, LANES))
            l_rep = jnp.broadcast_to(l, (stack, LANES))
            o4 = None
            for hi in range(HEADS_PER_SLAB):
                rows = slice(hi * Q_TILE, (hi + 1) * Q_TILE)
                part = _dot(pb[rows], vm_ref[hi, pl.ds(r0, K_TILE), cs])
                o4 = part if o4 is None else o4 + part
                head = hg * HEADS_PER_SLAB + hi
                stats = jnp.where(st_col == head, m_rep[rows], stats)
                stats = jnp.where(st_col == A_HEADS + head, l_rep[rows], stats)
            o2_ref[pl.ds(r0, Q_TILE), cs] = o4.astype(BF16)
        st2_ref[pl.ds(r0, Q_TILE), :] = stats

    for s in range(tq // Q_TILE):
        tile(s)
    o_ref[...] = o2_ref[...].reshape(o_ref.shape)
    st_ref[...] = st2_ref[...].reshape(st_ref.shape)


def _attn_group(qkv, group, band, *, batch, seq, tq=512):
    dil = A_GROUPS[group][1]
    tokens, width = qkv.shape
    phase_len = seq // dil
    tq = min(tq, phase_len)
    nq = phase_len // tq
    hb = phase_len // A_HALF
    r = tq // A_HALF
    run = PHASE_BLOCK // dil
    cq = group * 3
    cur_i = lambda b, i: b * nq + i
    prev_i = lambda b, i: b * hb + jnp.maximum(i * r - 1, 0)
    next_i = lambda b, i: b * hb + jnp.minimum((i + 1) * r, hb - 1)

    if dil == 1:
        view_shape = lambda cols: (tokens, cols)
        spec = lambda rows, cols, row_i, col_i: pl.BlockSpec(
            (rows, cols), lambda b, p, i: (row_i(b, i), col_i))
    else:
        view_shape = lambda cols: (tokens // PHASE_BLOCK, dil, run, cols)
        spec = lambda rows, cols, row_i, col_i: pl.BlockSpec(
            (rows // run, None, run, cols), lambda b, p, i: (row_i(b, i), p, 0, col_i))
    cur = lambda c: spec(tq, D_MODEL, cur_i, c)
    prev = lambda c: spec(A_HALF, D_MODEL, prev_i, c)
    nxt = lambda c: spec(A_HALF, D_MODEL, next_i, c)

    qkv_v = qkv.reshape(view_shape(width))
    o, stats = pl.pallas_call(
        functools.partial(_attn_kernel, tq=tq, phase_len=phase_len),
        grid=(batch, dil, nq),
        in_specs=[cur(cq), prev(cq + 1), cur(cq + 1), nxt(cq + 1), prev(cq + 2), cur(cq + 2), nxt(cq + 2),
                  pl.BlockSpec(band.shape, lambda b, p, i: (0, 0))],
        out_specs=[spec(tq, D_MODEL, cur_i, 0), spec(tq, LANES, cur_i, 0)],
        out_shape=[
            jax.ShapeDtypeStruct(view_shape(D_MODEL), BF16),
            jax.ShapeDtypeStruct(view_shape(LANES), F32),
        ],
        scratch_shapes=[pltpu.VMEM((tq, D_MODEL), BF16),
                        pltpu.VMEM((tq + 2 * A_HALF, D_MODEL), BF16),
                        pltpu.VMEM((HEADS_PER_SLAB, tq + 2 * A_HALF, D_MODEL), BF16),
                        pltpu.VMEM((tq, D_MODEL), BF16),
                        pltpu.VMEM((tq, LANES), F32)],
        compiler_params=_cparams(("parallel", "parallel", "parallel")),
        name=f"attn_group{group}",
    )(qkv_v, qkv_v, qkv_v, qkv_v, qkv_v, qkv_v, qkv_v, band)
    return o.reshape(tokens, D_MODEL), stats.reshape(tokens, LANES)


def _split3(x):
    hi = x.astype(BF16)
    r1 = x - hi.astype(F32)
    mid = r1.astype(BF16)
    return hi, mid, (r1 - mid.astype(F32)).astype(BF16)


def _a_out_kernel(o0_ref, o1_ref, o2_ref, s0_ref, s1_ref, s2_ref, exp_ref, unperm_ref, w_ref, h_ref, out_ref,
                  *, tm):
    for s in range(tm // PHASE_BLOCK):
        rows = slice(s * PHASE_BLOCK, (s + 1) * PHASE_BLOCK)
        outs = [o0_ref[rows, :].astype(F32)]
        stats = [s0_ref[rows, :]]
        for g, (o_ref, s_ref) in enumerate(((o1_ref, s1_ref), (o2_ref, s2_ref))):
            outs.append(_dot(unperm_ref[g], o_ref[rows, :]))
            stats.append(sum(_dot(unperm_ref[g], t) for t in _split3(s_ref[rows, :])))
        is_head = lax.broadcasted_iota(jnp.int32, (PHASE_BLOCK, LANES), 1) < A_HEADS
        dens = [jnp.where(is_head, pltpu.roll(st, LANES - A_HEADS, 1), 1.0) for st in stats]
        lses = [jnp.where(is_head, st * LN2 + jnp.log(d), 0.0) for st, d in zip(stats, dens)]
        m = jnp.maximum(jnp.maximum(lses[0], lses[1]), lses[2])
        es = [jnp.exp(x - m) for x in lses]
        inv = 1.0 / (es[0] + es[1] + es[2])
        mix = None
        for e, d, o in zip(es, dens, outs):
            wgt = jnp.where(is_head, e * inv / d, 0.0)
            hi = wgt.astype(BF16)
            lo = (wgt - hi.astype(F32)).astype(BF16)
            wide = _dot(jnp.concatenate([hi, lo], axis=1), exp_ref[...])
            mix = wide * o if mix is None else mix + wide * o
        out_ref[rows, :] = h_ref[rows, :] + _dot(mix.astype(BF16), w_ref[...])


def _a_out_proj(os_, stats, expand, unperms, w_out, h, *, tm=1024):
    tokens = h.shape[0]
    row = lambda i: (i, 0)
    const = lambda i: (0, 0)
    return pl.pallas_call(
        functools.partial(_a_out_kernel, tm=tm),
        grid=(tokens // tm,),
        in_specs=[pl.BlockSpec((tm, D_MODEL), row)] * 3 + [pl.BlockSpec((tm, LANES), row)] * 3 + [
            pl.BlockSpec((2 * LANES, D_MODEL), const),
            pl.BlockSpec(unperms.shape, lambda i: (0, 0, 0)),
            pl.BlockSpec((D_MODEL, D_MODEL), const),
            pl.BlockSpec((tm, D_MODEL), row),
        ],
        out_specs=pl.BlockSpec((tm, D_MODEL), row),
        out_shape=jax.ShapeDtypeStruct((tokens, D_MODEL), F32),
        compiler_params=_cparams(("parallel",)),
        name="a_out_proj",
    )(*os_, *stats, expand, unperms, w_out, h)


def _ffn_kernel(h_ref, g_ref, wgu_ref, wd_ref, out_ref, *, th):
    x = h_ref[...]
    xn = _rms_rows(x, g_ref[...]).astype(BF16)
    acc = x
    for c in range(FFN_HIDDEN // th):
        cols = slice(c * th, (c + 1) * th)
        gate = _dot(xn, wgu_ref[:, cols])
        up = _dot(xn, wgu_ref[:, FFN_HIDDEN + c * th:FFN_HIDDEN + (c + 1) * th])
        act = (gate * jax.nn.sigmoid(gate) * up).astype(BF16)
        acc = acc + _dot(act, wd_ref[cols, :])
    out_ref[...] = acc


def _ffn(h, gain, wgu, wd, *, tm=512, th=256):
    tokens = h.shape[0]
    return pl.pallas_call(
        functools.partial(_ffn_kernel, th=th),
        grid=(tokens // tm,),
        in_specs=[
            pl.BlockSpec((tm, D_MODEL), lambda i: (i, 0)),
            pl.BlockSpec((1, D_MODEL), lambda i: (0, 0)),
            pl.BlockSpec(wgu.shape, lambda i: (0, 0)),
            pl.BlockSpec(wd.shape, lambda i: (0, 0)),
        ],
        out_specs=pl.BlockSpec((tm, D_MODEL), lambda i: (i, 0)),
        out_shape=jax.ShapeDtypeStruct((tokens, D_MODEL), F32),
        compiler_params=_cparams(("parallel",)),
        name="ffn",
    )(h, gain, wgu, wd)


def _log_sigmoid(x):
    return jnp.minimum(x, 0.0) - jnp.log1p(jnp.exp(-jnp.abs(x)))


def _b_in_kernel(h_ref, ng_ref, wmain_ref, wvt_ref, wz_ref, wgf_ref, wgb_ref, bf_ref, bb_ref,
                 q_ref, k_ref, v_ref, vt_ref, sr_ref, laf_ref, lab_ref, *, tm):
    sub = PHASE_BLOCK
    for s in range(tm // sub):
        rows = slice(s * sub, (s + 1) * sub)
        xn = _rms_rows(h_ref[rows, :], ng_ref[...]).astype(BF16)
        y = _dot(xn, wmain_ref[...])
        q_ref[rows, :] = (y[:, :B_QK_WIDTH] * (B_KEY_DIM ** -0.5)).astype(BF16)
        k_ref[rows, :] = y[:, B_QK_WIDTH:2 * B_QK_WIDTH].astype(BF16)
        v_ref[rows, :] = y[:, 2 * B_QK_WIDTH:2 * B_QK_WIDTH + B_V_WIDTH].astype(BF16)
        r = y[:, 2 * B_QK_WIDTH + B_V_WIDTH:]
        sr_ref[rows, :] = (r * jax.nn.sigmoid(r)).astype(BF16)
        vt_ref[:, rows] = _nt_dot(wvt_ref[...], xn).astype(BF16)
        z = _dot(xn, wz_ref[...]).astype(BF16)
        laf_ref[rows, :] = _log_sigmoid(_dot(z, wgf_ref[...]) + bf_ref[...]) * (1.0 / B_GATE_TAU)
        lab_ref[rows, :] = _log_sigmoid(_dot(z, wgb_ref[...]) + bb_ref[...]) * (1.0 / B_GATE_TAU)


def _b_in_proj(h, norm_gain, wmain, wvt, wz, wgf, wgb, bias_f, bias_b, *, tm=512):
    tokens = h.shape[0]
    row = lambda i: (i, 0)
    const = lambda i: (0, 0)
    full = lambda a: pl.BlockSpec(a.shape, const)
    return pl.pallas_call(
        functools.partial(_b_in_kernel, tm=tm),
        grid=(tokens // tm,),
        in_specs=[pl.BlockSpec((tm, D_MODEL), row)] + [full(a) for a in
                  (norm_gain, wmain, wvt, wz, wgf, wgb, bias_f, bias_b)],
        out_specs=[
            pl.BlockSpec((tm, B_QK_WIDTH), row),
            pl.BlockSpec((tm, B_QK_WIDTH), row),
            pl.BlockSpec((tm, B_V_WIDTH), row),
            pl.BlockSpec((B_V_WIDTH, tm), lambda i: (0, i)),
            pl.BlockSpec((tm, B_V_WIDTH), row),
            pl.BlockSpec((tm, B_QK_WIDTH), row),
            pl.BlockSpec((tm, B_QK_WIDTH), row),
        ],
        out_shape=[
            jax.ShapeDtypeStruct((tokens, B_QK_WIDTH), BF16),
            jax.ShapeDtypeStruct((tokens, B_QK_WIDTH), BF16),
            jax.ShapeDtypeStruct((tokens, B_V_WIDTH), BF16),
            jax.ShapeDtypeStruct((B_V_WIDTH, tokens), BF16),
            jax.ShapeDtypeStruct((tokens, B_V_WIDTH), BF16),
            jax.ShapeDtypeStruct((tokens, B_QK_WIDTH), F32),
            jax.ShapeDtypeStruct((tokens, B_QK_WIDTH), F32),
        ],
        compiler_params=_cparams(("parallel",)),
        name="b_in_proj",
    )(h, norm_gain, wmain, wvt, wz, wgf, wgb, bias_f, bias_b)


def _gla_block(q_ref, k_ref, v_ref, vt_ref, la_ref, tri_ref, st_ref, o_ref, scr, *, cb, reverse):
    qt_ref, kt_ref, ke_ref, dec_ref, kv_ref, sin_ref = scr
    n_ch = cb // B_CHUNK
    pair_w = 2 * B_CHUNK
    row = lax.broadcasted_iota(jnp.int32, (B_CHUNK, B_CHUNK), 0)
    col = lax.broadcasted_iota(jnp.int32, (B_CHUNK, B_CHUNK), 1)
    amask = (col > row) if reverse else (col <= row)
    heads = [(slice(hh * B_KEY_DIM, (hh + 1) * B_KEY_DIM), slice(hh * B_VAL_DIM, (hh + 1) * B_VAL_DIM))
             for hh in range(B_HEADS)]
    chunk_rows = lambda c: slice(c * B_CHUNK, (c + 1) * B_CHUNK)

    for c in range(n_ch):
        rows = chunk_rows(c)
        g = la_ref[rows, :]
        g_hi = g.astype(BF16)
        g_lo = (g - g_hi.astype(F32)).astype(BF16)
        b = _dot(tri_ref[...], g_hi) + _dot(tri_ref[...], g_lo)
        dec = jnp.exp(b[0:1, :] if reverse else b[B_CHUNK - 1:B_CHUNK, :])
        dec_ref[c:c + 1, :] = dec
        qt_ref[rows, :] = (q_ref[rows, :].astype(F32) * jnp.exp(b)).astype(BF16)
        kt_f = k_ref[rows, :].astype(F32) * jnp.exp(-b)
        kt_ref[rows, :] = kt_f.astype(BF16)
        ke_ref[rows, :] = (kt_f * dec).astype(BF16)

    for c in range(n_ch):
        rows = chunk_rows(c)
        for sk, sv in heads:
            att = jnp.where(amask, _nt_dot(qt_ref[rows, sk], kt_ref[rows, sk]), 0.0)
            o_ref[rows, sv] = _dot(att.astype(BF16), v_ref[rows, sv])

    zero = jnp.zeros((B_CHUNK, B_KEY_DIM), BF16)
    for pair in range(n_ch // 2):
        r0, r1 = chunk_rows(2 * pair), chunk_rows(2 * pair + 1)
        for hh, (sk, sv) in enumerate(heads):
            rhs = jnp.concatenate([jnp.concatenate([ke_ref[r0, sk], zero], axis=1),
                                   jnp.concatenate([zero, ke_ref[r1, sk]], axis=1)], axis=0)
            kv2 = _dot(vt_ref[sv, pair * pair_w:(pair + 1) * pair_w], rhs)
            kv_ref[2 * pair, hh] = kv2[:, :B_KEY_DIM]
            kv_ref[2 * pair + 1, hh] = kv2[:, B_KEY_DIM:]

    for step in range(n_ch):
        c = n_ch - 1 - step if reverse else step
        for hh, (sk, _) in enumerate(heads):
            st = st_ref[hh]
            sin_ref[c, hh] = st.astype(BF16)
            st_ref[hh] = st * dec_ref[c:c + 1, sk] + kv_ref[c, hh]

    for c in range(n_ch):
        rows = chunk_rows(c)
        for hh, (sk, sv) in enumerate(heads):
            o_ref[rows, sv] = o_ref[rows, sv] + _nt_dot(qt_ref[rows, sk], sin_ref[c, hh])


def _gla_scratch(cb):
    n_ch = cb // B_CHUNK
    return [pltpu.VMEM((cb, B_QK_WIDTH), BF16), pltpu.VMEM((cb, B_QK_WIDTH), BF16),
            pltpu.VMEM((cb, B_QK_WIDTH), BF16), pltpu.VMEM((n_ch, B_QK_WIDTH), F32),
            pltpu.VMEM((n_ch, B_HEADS, B_VAL_DIM, B_KEY_DIM), F32),
            pltpu.VMEM((n_ch, B_HEADS, B_VAL_DIM, B_KEY_DIM), BF16)]


def _gla_fwd_kernel(q_ref, k_ref, v_ref, vt_ref, la_ref, tri_ref, o_ref, st_ref, *scr, cb):
    @pl.when(pl.program_id(1) == 0)
    def _():
        st_ref[...] = jnp.zeros_like(st_ref)

    _gla_block(q_ref, k_ref, v_ref, vt_ref, la_ref, tri_ref, st_ref, o_ref, scr, cb=cb, reverse=False)


def _gla_bwd_out_kernel(q_ref, k_ref, v_ref, vt_ref, la_ref, tri_ref, of_ref, sr_ref, og_ref, w_ref,
                        h_ref, out_ref, st_ref, ob_ref, *scr, cb):
    @pl.when(pl.program_id(1) == 0)
    def _():
        st_ref[...] = jnp.zeros_like(st_ref)

    _gla_block(q_ref, k_ref, v_ref, vt_ref, la_ref, tri_ref, st_ref, ob_ref, scr, cb=cb, reverse=True)
    ys = []
    for hh in range(B_HEADS):
        sv = slice(hh * B_VAL_DIM, (hh + 1) * B_VAL_DIM)
        o = of_ref[:, sv] + ob_ref[:, sv]
        ys.append((_rms_rows(o, og_ref[:, sv]) * sr_ref[:, sv].astype(F32)).astype(BF16))
    out_ref[...] = h_ref[...] + _dot(jnp.concatenate(ys, axis=1), w_ref[...])


def _gla_specs(batch, seq, cb, reverse):
    nb = seq // cb
    blk = (lambda b, i: b * nb + (nb - 1 - i)) if reverse else (lambda b, i: b * nb + i)
    row = lambda width: pl.BlockSpec((cb, width), lambda b, i: (blk(b, i), 0))
    const = lambda shape: pl.BlockSpec(shape, lambda b, i: (0,) * len(shape))
    vt = pl.BlockSpec((B_V_WIDTH, cb), lambda b, i: (0, blk(b, i)))
    return nb, row, const, vt


def _gla_fwd(q, k, v, vt, la, tri, *, batch, seq, cb=512):
    cb = min(cb, seq)
    nb, row, const, vt_spec = _gla_specs(batch, seq, cb, False)
    return pl.pallas_call(
        functools.partial(_gla_fwd_kernel, cb=cb),
        grid=(batch, nb),
        in_specs=[row(B_QK_WIDTH), row(B_QK_WIDTH), row(B_V_WIDTH), vt_spec, row(B_QK_WIDTH),
                  const((B_CHUNK, B_CHUNK))],
        out_specs=row(B_V_WIDTH),
        out_shape=jax.ShapeDtypeStruct((batch * seq, B_V_WIDTH), F32),
        scratch_shapes=[pltpu.VMEM((B_HEADS, B_VAL_DIM, B_KEY_DIM), F32)] + _gla_scratch(cb),
        compiler_params=_cparams(("parallel", "arbitrary")),
        name="gla_fwd",
    )(q, k, v, vt, la, tri)


def _gla_bwd_out(q, k, v, vt, la, tri, o_f, silu_r, out_gain, w_out, h, *, batch, seq, cb=512):
    cb = min(cb, seq)
    nb, row, const, vt_spec = _gla_specs(batch, seq, cb, True)
    return pl.pallas_call(
        functools.partial(_gla_bwd_out_kernel, cb=cb),
        grid=(batch, nb),
        in_specs=[row(B_QK_WIDTH), row(B_QK_WIDTH), row(B_V_WIDTH), vt_spec, row(B_QK_WIDTH),
                  const((B_CHUNK, B_CHUNK)), row(B_V_WIDTH), row(B_V_WIDTH), const((1, B_V_WIDTH)),
                  const((B_V_WIDTH, D_MODEL)), row(D_MODEL)],
        out_specs=row(D_MODEL),
        out_shape=jax.ShapeDtypeStruct((batch * seq, D_MODEL), F32),
        scratch_shapes=[pltpu.VMEM((B_HEADS, B_VAL_DIM, B_KEY_DIM), F32),
                        pltpu.VMEM((cb, B_V_WIDTH), F32)] + _gla_scratch(cb),
        compiler_params=_cparams(("parallel", "arbitrary")),
        name="gla_bwd_out",
    )(q, k, v, vt, la, tri, o_f, silu_r, out_gain, w_out, h)


def _a_layer_params(w_in, q_gain, k_gain):
    n_g, n_slab, half = len(A_GROUPS), D_MODEL // SLAB, A_HEAD_DIM // 2
    w7 = w_in.reshape(D_MODEL, n_g, 3, n_slab, HEADS_PER_SLAB, 2, half)
    w_qk = w7[:, :, :2].transpose(0, 1, 2, 3, 5, 4, 6).reshape(D_MODEL, n_g, 2, D_MODEL)
    w_v = w7[:, :, 2:].reshape(D_MODEL, n_g, 1, D_MODEL)
    w_perm = jnp.concatenate([w_qk, w_v], axis=2).astype(BF16)
    w_perm = w_perm.transpose(1, 2, 0, 3).reshape(n_g * 3, D_MODEL, D_MODEL)

    def spread(gain):
        g5 = jnp.broadcast_to(gain.reshape(n_g, 1, 2, 1, half), (n_g, n_slab, 2, HEADS_PER_SLAB, half))
        return g5.reshape(n_g, 1, D_MODEL)

    gains = jnp.concatenate([spread(q_gain), spread(k_gain), jnp.ones((n_g, 1, D_MODEL), F32)], axis=1)
    return w_perm, gains.reshape(1, n_g * 3 * D_MODEL).astype(F32)


def _phase_major_index(dil):
    r = np.arange(PHASE_BLOCK)
    run = PHASE_BLOCK // dil
    return (r % run) * dil + r // run


def _rope_tables(seq):
    half = A_HEAD_DIM // 2
    inv_freq = ROPE_THETA ** (-jnp.arange(half, dtype=F32) / half)
    blocks = np.arange(seq // PHASE_BLOCK)[:, None] * PHASE_BLOCK
    pos = np.stack([(blocks + _phase_major_index(d)[None, :]).reshape(seq) for _, d in A_GROUPS])
    ang = jnp.asarray(pos, F32)[:, :, None] * inv_freq[None, None, :]
    reps = LANES // half
    return jnp.tile(jnp.cos(ang), (1, 1, reps)), jnp.tile(jnp.sin(ang), (1, 1, reps))


def _phase_perms():
    fwd = np.stack([np.eye(PHASE_BLOCK)[_phase_major_index(d)] for _, d in A_GROUPS[1:]])
    return jnp.asarray(fwd, BF16), jnp.asarray(fwd.transpose(0, 2, 1), BF16)


def _band_bias():
    r = np.arange(HEADS_PER_SLAB * Q_TILE)[:, None] % Q_TILE
    c = np.arange(K_TILE)[None, :]
    return jnp.asarray(np.where((c >= r) & (c <= r + 2 * A_HALF), 0.0, NEG_INF), F32)


def _segment_ones():
    a = np.arange(2 * LANES)
    return jnp.asarray((a[:, None] // 32) == (a[None, :] // 32), BF16)


def _head_expand():
    a = np.arange(2 * LANES) % LANES
    n = np.arange(D_MODEL) // A_HEAD_DIM
    return jnp.asarray(a[:, None] == n[None, :], BF16)


def _tri(reverse):
    r = np.arange(B_CHUNK)
    m = (r[None, :] >= r[:, None]) if reverse else (r[None, :] <= r[:, None])
    return jnp.asarray(m, BF16)


def _b_layer_params(w_in, w_gate_f, w_gate_b):
    n_main = 2 * B_QK_WIDTH + 2 * B_V_WIDTH
    wmain = w_in[:, :n_main].astype(BF16)
    wvt = w_in[:, 2 * B_QK_WIDTH:2 * B_QK_WIDTH + B_V_WIDTH].T.astype(BF16)
    wz = jnp.pad(w_in[:, n_main:], ((0, 0), (0, LANES - 2 * B_GATE_RANK))).astype(BF16)
    wgf = jnp.pad(w_gate_f, ((0, LANES - B_GATE_RANK), (0, 0))).astype(BF16)
    wgb = jnp.pad(w_gate_b, ((B_GATE_RANK, LANES - 2 * B_GATE_RANK), (0, 0))).astype(BF16)
    return wmain, wvt, wz, wgf, wgb


def _mixer_a(h, norm_gain, w_in, q_gain, k_gain, w_out, tables, *, batch, seq):
    cos_t, sin_t, seg, expand, perms, unperms, band = tables
    w_perm, qk_gain = _a_layer_params(w_in, q_gain, k_gain)
    qkv = _a_in_proj(h, norm_gain, w_perm, qk_gain, cos_t, sin_t, seg, perms, seq=seq)
    outs = [_attn_group(qkv, g, band, batch=batch, seq=seq) for g in range(len(A_GROUPS))]
    return _a_out_proj([o for o, _ in outs], [s for _, s in outs], expand, unperms, w_out.astype(BF16), h)


def _mixer_b(h, norm_gain, w_in, w_gate_f, bias_f, w_gate_b, bias_b, out_gain, w_out, *, batch, seq):
    wmain, wvt, wz, wgf, wgb = _b_layer_params(w_in, w_gate_f, w_gate_b)
    q, k, v, vt, silu_r, la_f, la_b = _b_in_proj(h, norm_gain, wmain, wvt, wz, wgf, wgb,
                                                bias_f[None, :], bias_b[None, :])
    o_f = _gla_fwd(q, k, v, vt, la_f, _tri(False), batch=batch, seq=seq)
    return _gla_bwd_out(q, k, v, vt, la_b, _tri(True), o_f, silu_r, out_gain.reshape(1, B_V_WIDTH),
                        w_out.astype(BF16), h, batch=batch, seq=seq)


def kernel(x, attn_norm, ffn_norm, a_w_in, a_q_norm, a_k_norm, a_w_out, b_w_in, b_w_gate_f, b_gate_bias_f,
           b_w_gate_b, b_gate_bias_b, b_out_norm, b_w_out, ffn_w_gate_up, ffn_w_down):
    batch, seq, _ = x.shape
    h = x.reshape(batch * seq, D_MODEL)
    tables = _rope_tables(seq) + (_segment_ones(), _head_expand()) + _phase_perms() + (_band_bias(),)
    for i in range(DEPTH):
        j = i // 2
        if i % 2 == 0:
            h = _mixer_a(h, attn_norm[i][None, :], a_w_in[j], a_q_norm[j], a_k_norm[j], a_w_out[j], tables,
                         batch=batch, seq=seq)
        else:
            h = _mixer_b(h, attn_norm[i][None, :], b_w_in[j], b_w_gate_f[j], b_gate_bias_f[j], b_w_gate_b[j],
                         b_gate_bias_b[j], b_out_norm[j], b_w_out[j], batch=batch, seq=seq)
        h = _ffn(h, ffn_norm[i][None, :], ffn_w_gate_up[i].astype(BF16), ffn_w_down[i].astype(BF16))
    return h.reshape(batch, seq, D_MODEL)
```

```python
import functools

import numpy as np
import jax
import jax.numpy as jnp
from jax import lax
from jax.experimental import pallas as pl
from jax.experimental.pallas import tpu as pltpu

F32 = jnp.float32
BF16 = jnp.bfloat16

D_MODEL = 1024
DEPTH = 4
RMS_EPS = 1e-6
NEG_INF = -1e30

A_GROUPS = ((128, 1), (512, 4), (2048, 16))
A_HEADS = 16
A_HEAD_DIM = 64
A_HALF = 64
ROPE_THETA = 10000.0
LOG2E = 1.4426950408889634
LN2 = 0.6931471805599453
Q_SCALE = A_HEAD_DIM ** -0.5 * LOG2E

B_HEADS = 4
B_KEY_DIM = 128
B_VAL_DIM = 256
B_QK_WIDTH = B_HEADS * B_KEY_DIM
B_V_WIDTH = B_HEADS * B_VAL_DIM
B_GATE_RANK = 16
B_GATE_TAU = 16.0
B_CHUNK = 64

FFN_HIDDEN = 2816

LANES = 128
V7X_VMEM_BYTES = 64 * 1024 * 1024
VMEM_LIMIT_BYTES = V7X_VMEM_BYTES - 8 * 1024 * 1024

Q_TILE = 128
K_TILE = Q_TILE + 2 * A_HALF
HEADS_PER_SLAB = 4
SLAB = HEADS_PER_SLAB * A_HEAD_DIM
PHASE_BLOCK = 256

for _w, _d in A_GROUPS:
    assert _w // (2 * _d) == A_HALF


def _cparams(sem):
    return pltpu.CompilerParams(dimension_semantics=sem, vmem_limit_bytes=VMEM_LIMIT_BYTES)


def _nt_dot(a, b):
    return lax.dot_general(a, b, (((1,), (1,)), ((), ())), preferred_element_type=F32)


def _dot(a, b):
    return jnp.dot(a, b, preferred_element_type=F32)


def _rms_rows(x, gain):
    ms = jnp.mean(x * x, axis=-1, keepdims=True)
    return x * lax.rsqrt(ms + RMS_EPS) * gain


def _a_in_kernel(h_ref, ng_ref, w_ref, qkg_ref, cos_ref, sin_ref, seg_ref, perm_ref, out_ref, xn_ref, xnat_ref,
                 *, tm):
    g = pl.program_id(1)
    sub = PHASE_BLOCK

    @pl.when(g == 0)
    def _():
        xn = _rms_rows(h_ref[...], ng_ref[...]).astype(BF16)
        xn_ref[...] = xn
        xnat_ref[...] = xn

    @pl.when(g > 0)
    def _():
        perm = perm_ref[jnp.maximum(g - 1, 0)]
        for s in range(tm // sub):
            rows = slice(s * sub, (s + 1) * sub)
            xn_ref[rows, :] = _dot(perm, xnat_ref[rows, :]).astype(BF16)

    for s in range(tm // sub):
        rows = slice(s * sub, (s + 1) * sub)
        x = xn_ref[rows, :]
        cos = cos_ref[rows, :]
        sin = sin_ref[rows, :]
        for chunk, scale in ((0, Q_SCALE), (1, 1.0)):
            acc = _dot(x, w_ref[3 * g + chunk])
            base = chunk * D_MODEL
            for pair in range(D_MODEL // (2 * SLAB)):
                ab = []
                for g4 in (2 * pair, 2 * pair + 1):
                    c0 = g4 * SLAB
                    ab.append((acc[:, c0:c0 + LANES], acc[:, c0 + LANES:c0 + SLAB]))
                sq = jnp.concatenate([a * a + b * b for a, b in ab], axis=1).astype(BF16)
                ss = _dot(sq, seg_ref[...])
                for k, (a, b) in enumerate(ab):
                    c0 = base + (2 * pair + k) * SLAB
                    r = lax.rsqrt(ss[:, k * LANES:(k + 1) * LANES] * (1.0 / A_HEAD_DIM) + RMS_EPS) * scale
                    an = a * r * qkg_ref[:, c0:c0 + LANES]
                    bn = b * r * qkg_ref[:, c0 + LANES:c0 + SLAB]
                    out_ref[rows, c0:c0 + LANES] = (an * cos - bn * sin).astype(BF16)
                    out_ref[rows, c0 + LANES:c0 + SLAB] = (bn * cos + an * sin).astype(BF16)
        out_ref[rows, 2 * D_MODEL:] = _dot(x, w_ref[3 * g + 2]).astype(BF16)


def _a_in_proj(h, norm_gain, w_perm, qk_gain, cos_t, sin_t, seg, perms, *, seq, tm=1024):
    tokens = h.shape[0]
    n_groups = w_perm.shape[0] // 3
    tm = min(tm, seq)
    pos_blocks = seq // tm
    table = pl.BlockSpec((None, tm, LANES), lambda i, g: (g, i % pos_blocks, 0))
    return pl.pallas_call(
        functools.partial(_a_in_kernel, tm=tm),
        grid=(tokens // tm, n_groups),
        in_specs=[
            pl.BlockSpec((tm, D_MODEL), lambda i, g: (i, 0)),
            pl.BlockSpec((1, D_MODEL), lambda i, g: (0, 0)),
            pl.BlockSpec(w_perm.shape, lambda i, g: (0, 0, 0), pipeline_mode=pl.Buffered(1)),
            pl.BlockSpec((1, 3 * D_MODEL), lambda i, g: (0, g)),
            table,
            table,
            pl.BlockSpec(seg.shape, lambda i, g: (0, 0)),
            pl.BlockSpec(perms.shape, lambda i, g: (0, 0, 0)),
        ],
        out_specs=pl.BlockSpec((tm, 3 * D_MODEL), lambda i, g: (i, g)),
        out_shape=jax.ShapeDtypeStruct((tokens, n_groups * 3 * D_MODEL), BF16),
        scratch_shapes=[pltpu.VMEM((tm, D_MODEL), BF16), pltpu.VMEM((tm, D_MODEL), BF16)],
        compiler_params=_cparams(("parallel", "arbitrary")),
        name="a_in_proj",
    )(h, norm_gain, w_perm, qk_gain, cos_t, sin_t, seg, perms)


def _attn_kernel(q_ref, kp_ref, kc_ref, kn_ref, vp_ref, vc_ref, vn_ref, band_ref, o_ref, st_ref,
                 q2_ref, kcat_ref, vm_ref, o2_ref, st2_ref, *, tq, phase_len):
    i = pl.program_id(2)
    flat = lambda ref: ref[...].reshape(-1, ref.shape[-1])
    q2_ref[...] = flat(q_ref)
    kcat_ref[0:A_HALF, :] = flat(kp_ref)
    kcat_ref[A_HALF:A_HALF + tq, :] = flat(kc_ref)
    kcat_ref[A_HALF + tq:2 * A_HALF + tq, :] = flat(kn_ref)
    lane = lax.broadcasted_iota(jnp.int32, (1, D_MODEL), 1)
    for hi in range(HEADS_PER_SLAB):
        keep = (((lane & (SLAB - 1)) >> 6) == hi).astype(F32).astype(BF16)
        vm_ref[hi, 0:A_HALF, :] = flat(vp_ref) * keep
        vm_ref[hi, A_HALF:A_HALF + tq, :] = flat(vc_ref) * keep
        vm_ref[hi, A_HALF + tq:2 * A_HALF + tq, :] = flat(vn_ref) * keep

    stack = HEADS_PER_SLAB * Q_TILE
    key_col = lax.broadcasted_iota(jnp.int32, (1, K_TILE), 1)
    col1 = lax.broadcasted_iota(jnp.int32, (Q_TILE, K_TILE), 1)
    q_head = (col1 & (LANES - 1)) >> 5
    q_keep = [(q_head == hi).astype(F32).astype(BF16) for hi in range(HEADS_PER_SLAB)]
    st_col = lax.broadcasted_iota(jnp.int32, (Q_TILE, LANES), 1)

    def tile(s):
        r0 = s * Q_TILE
        key_pos = i * tq + s * Q_TILE - A_HALF + key_col
        in_seq = (key_pos >= 0) & (key_pos < phase_len)
        bias = band_ref[...] + jnp.where(in_seq, 0.0, NEG_INF)
        stats = jnp.zeros((Q_TILE, LANES), F32)
        for hg in range(D_MODEL // SLAB):
            cs = slice(hg * SLAB, (hg + 1) * SLAB)
            q4 = q2_ref[pl.ds(r0, Q_TILE), cs]
            k4 = kcat_ref[pl.ds(r0, K_TILE), cs]
            qm = jnp.concatenate([q4 * q_keep[hi] for hi in range(HEADS_PER_SLAB)], axis=0)
            sc = _nt_dot(qm, k4) + bias
            m = jnp.max(sc, axis=-1, keepdims=True)
            p = jnp.exp2(sc - m)
            l = jnp.sum(p, axis=-1, keepdims=True)
            pb = p.astype(BF16)
            m_rep = jnp.broadcast_to(m, (stack, LANES))
            l_rep = jnp.broadcast_to(l, (stack, LANES))
            o4 = None
            for hi in range(HEADS_PER_SLAB):
                rows = slice(hi * Q_TILE, (hi + 1) * Q_TILE)
                part = _dot(pb[rows], vm_ref[hi, pl.ds(r0, K_TILE), cs])
                o4 = part if o4 is None else o4 + part
                head = hg * HEADS_PER_SLAB + hi
                stats = jnp.where(st_col == head, m_rep[rows], stats)
                stats = jnp.where(st_col == A_HEADS + head, l_rep[rows], stats)
            o2_ref[pl.ds(r0, Q_TILE), cs] = o4.astype(BF16)
        st2_ref[pl.ds(r0, Q_TILE), :] = stats

    for s in range(tq // Q_TILE):
        tile(s)
    o_ref[...] = o2_ref[...].reshape(o_ref.shape)
    st_ref[...] = st2_ref[...].reshape(st_ref.shape)


def _attn_group(qkv, group, band, *, batch, seq, tq=512):
    dil = A_GROUPS[group][1]
    tokens, width = qkv.shape
    phase_len = seq // dil
    tq = min(tq, phase_len)
    nq = phase_len // tq
    hb = phase_len // A_HALF
    r = tq // A_HALF
    run = PHASE_BLOCK // dil
    cq = group * 3
    cur_i = lambda b, i: b * nq + i
    prev_i = lambda b, i: b * hb + jnp.maximum(i * r - 1, 0)
    next_i = lambda b, i: b * hb + jnp.minimum((i + 1) * r, hb - 1)

    if dil == 1:
        view_shape = lambda cols: (tokens, cols)
        spec = lambda rows, cols, row_i, col_i: pl.BlockSpec(
            (rows, cols), lambda b, p, i: (row_i(b, i), col_i))
    else:
        view_shape = lambda cols: (tokens // PHASE_BLOCK, dil, run, cols)
        spec = lambda rows, cols, row_i, col_i: pl.BlockSpec(
            (rows // run, None, run, cols), lambda b, p, i: (row_i(b, i), p, 0, col_i))
    cur = lambda c: spec(tq, D_MODEL, cur_i, c)
    prev = lambda c: spec(A_HALF, D_MODEL, prev_i, c)
    nxt = lambda c: spec(A_HALF, D_MODEL, next_i, c)

    qkv_v = qkv.reshape(view_shape(width))
    o, stats = pl.pallas_call(
        functools.partial(_attn_kernel, tq=tq, phase_len=phase_len),
        grid=(batch, dil, nq),
        in_specs=[cur(cq), prev(cq + 1), cur(cq + 1), nxt(cq + 1), prev(cq + 2), cur(cq + 2), nxt(cq + 2),
                  pl.BlockSpec(band.shape, lambda b, p, i: (0, 0))],
        out_specs=[spec(tq, D_MODEL, cur_i, 0), spec(tq, LANES, cur_i, 0)],
        out_shape=[
            jax.ShapeDtypeStruct(view_shape(D_MODEL), BF16),
            jax.ShapeDtypeStruct(view_shape(LANES), F32),
        ],
        scratch_shapes=[pltpu.VMEM((tq, D_MODEL), BF16),
                        pltpu.VMEM((tq + 2 * A_HALF, D_MODEL), BF16),
                        pltpu.VMEM((HEADS_PER_SLAB, tq + 2 * A_HALF, D_MODEL), BF16),
                        pltpu.VMEM((tq, D_MODEL), BF16),
                        pltpu.VMEM((tq, LANES), F32)],
        compiler_params=_cparams(("parallel", "parallel", "parallel")),
        name=f"attn_group{group}",
    )(qkv_v, qkv_v, qkv_v, qkv_v, qkv_v, qkv_v, qkv_v, band)
    return o.reshape(tokens, D_MODEL), stats.reshape(tokens, LANES)


def _split3(x):
    hi = x.astype(BF16)
    r1 = x - hi.astype(F32)
    mid = r1.astype(BF16)
    return hi, mid, (r1 - mid.astype(F32)).astype(BF16)


def _a_out_kernel(o0_ref, o1_ref, o2_ref, s0_ref, s1_ref, s2_ref, exp_ref, unperm_ref, w_ref, h_ref, out_ref,
                  *, tm):
    for s in range(tm // PHASE_BLOCK):
        rows = slice(s * PHASE_BLOCK, (s + 1) * PHASE_BLOCK)
        outs = [o0_ref[rows, :].astype(F32)]
        stats = [s0_ref[rows, :]]
        for g, (o_ref, s_ref) in enumerate(((o1_ref, s1_ref), (o2_ref, s2_ref))):
            outs.append(_dot(unperm_ref[g], o_ref[rows, :]))
            stats.append(sum(_dot(unperm_ref[g], t) for t in _split3(s_ref[rows, :])))
        is_head = lax.broadcasted_iota(jnp.int32, (PHASE_BLOCK, LANES), 1) < A_HEADS
        dens = [jnp.where(is_head, pltpu.roll(st, LANES - A_HEADS, 1), 1.0) for st in stats]
        lses = [jnp.where(is_head, st * LN2 + jnp.log(d), 0.0) for st, d in zip(stats, dens)]
        m = jnp.maximum(jnp.maximum(lses[0], lses[1]), lses[2])
        es = [jnp.exp(x - m) for x in lses]
        inv = 1.0 / (es[0] + es[1] + es[2])
        mix = None
        for e, d, o in zip(es, dens, outs):
            wgt = jnp.where(is_head, e * inv / d, 0.0)
            hi = wgt.astype(BF16)
            lo = (wgt - hi.astype(F32)).astype(BF16)
            wide = _dot(jnp.concatenate([hi, lo], axis=1), exp_ref[...])
            mix = wide * o if mix is None else mix + wide * o
        out_ref[rows, :] = h_ref[rows, :] + _dot(mix.astype(BF16), w_ref[...])


def _a_out_proj(os_, stats, expand, unperms, w_out, h, *, tm=1024):
    tokens = h.shape[0]
    row = lambda i: (i, 0)
    const = lambda i: (0, 0)
    return pl.pallas_call(
        functools.partial(_a_out_kernel, tm=tm),
        grid=(tokens // tm,),
        in_specs=[pl.BlockSpec((tm, D_MODEL), row)] * 3 + [pl.BlockSpec((tm, LANES), row)] * 3 + [
            pl.BlockSpec((2 * LANES, D_MODEL), const),
            pl.BlockSpec(unperms.shape, lambda i: (0, 0, 0)),
            pl.BlockSpec((D_MODEL, D_MODEL), const),
            pl.BlockSpec((tm, D_MODEL), row),
        ],
        out_specs=pl.BlockSpec((tm, D_MODEL), row),
        out_shape=jax.ShapeDtypeStruct((tokens, D_MODEL), F32),
        compiler_params=_cparams(("parallel",)),
        name="a_out_proj",
    )(*os_, *stats, expand, unperms, w_out, h)


def _ffn_kernel(h_ref, g_ref, wgu_ref, wd_ref, out_ref, *, th):
    x = h_ref[...]
    xn = _rms_rows(x, g_ref[...]).astype(BF16)
    acc = x
    for c in range(FFN_HIDDEN // th):
        cols = slice(c * th, (c + 1) * th)
        gate = _dot(xn, wgu_ref[:, cols])
        up = _dot(xn, wgu_ref[:, FFN_HIDDEN + c * th:FFN_HIDDEN + (c + 1) * th])
        act = (gate * jax.nn.sigmoid(gate) * up).astype(BF16)
        acc = acc + _dot(act, wd_ref[cols, :])
    out_ref[...] = acc


def _ffn(h, gain, wgu, wd, *, tm=512, th=256):
    tokens = h.shape[0]
    return pl.pallas_call(
        functools.partial(_ffn_kernel, th=th),
        grid=(tokens // tm,),
        in_specs=[
            pl.BlockSpec((tm, D_MODEL), lambda i: (i, 0)),
            pl.BlockSpec((1, D_MODEL), lambda i: (0, 0)),
            pl.BlockSpec(wgu.shape, lambda i: (0, 0)),
            pl.BlockSpec(wd.shape, lambda i: (0, 0)),
        ],
        out_specs=pl.BlockSpec((tm, D_MODEL), lambda i: (i, 0)),
        out_shape=jax.ShapeDtypeStruct((tokens, D_MODEL), F32),
        compiler_params=_cparams(("parallel",)),
        name="ffn",
    )(h, gain, wgu, wd)


def _log_sigmoid(x):
    return jnp.minimum(x, 0.0) - jnp.log1p(jnp.exp(-jnp.abs(x)))


def _b_in_kernel(h_ref, ng_ref, wmain_ref, wz_ref, wgf_ref, wgb_ref, bf_ref, bb_ref,
                 q_ref, k_ref, v_ref, vt_ref, sr_ref, laf_ref, lab_ref, *, tm):
    sub = PHASE_BLOCK
    for s in range(tm // sub):
        rows = slice(s * sub, (s + 1) * sub)
        xn = _rms_rows(h_ref[rows, :], ng_ref[...]).astype(BF16)
        y = _dot(xn, wmain_ref[...])
        q_ref[rows, :] = (y[:, :B_QK_WIDTH] * (B_KEY_DIM ** -0.5)).astype(BF16)
        k_ref[rows, :] = y[:, B_QK_WIDTH:2 * B_QK_WIDTH].astype(BF16)
        v = y[:, 2 * B_QK_WIDTH:2 * B_QK_WIDTH + B_V_WIDTH]
        v_ref[rows, :] = v.astype(BF16)
        vt_ref[:, rows] = v.T.astype(BF16)
        r = y[:, 2 * B_QK_WIDTH + B_V_WIDTH:]
        sr_ref[rows, :] = (r * jax.nn.sigmoid(r)).astype(BF16)
        z = _dot(xn, wz_ref[...]).astype(BF16)
        laf_ref[rows, :] = _log_sigmoid(_dot(z, wgf_ref[...]) + bf_ref[...]) * (1.0 / B_GATE_TAU)
        lab_ref[rows, :] = _log_sigmoid(_dot(z, wgb_ref[...]) + bb_ref[...]) * (1.0 / B_GATE_TAU)


def _b_in_proj(h, norm_gain, wmain, wz, wgf, wgb, bias_f, bias_b, *, tm=512):
    tokens = h.shape[0]
    row = lambda i: (i, 0)
    const = lambda i: (0, 0)
    full = lambda a: pl.BlockSpec(a.shape, const)
    return pl.pallas_call(
        functools.partial(_b_in_kernel, tm=tm),
        grid=(tokens // tm,),
        in_specs=[pl.BlockSpec((tm, D_MODEL), row)] + [full(a) for a in
                  (norm_gain, wmain, wz, wgf, wgb, bias_f, bias_b)],
        out_specs=[
            pl.BlockSpec((tm, B_QK_WIDTH), row),
            pl.BlockSpec((tm, B_QK_WIDTH), row),
            pl.BlockSpec((tm, B_V_WIDTH), row),
            pl.BlockSpec((B_V_WIDTH, tm), lambda i: (0, i)),
            pl.BlockSpec((tm, B_V_WIDTH), row),
            pl.BlockSpec((tm, B_QK_WIDTH), row),
            pl.BlockSpec((tm, B_QK_WIDTH), row),
        ],
        out_shape=[
            jax.ShapeDtypeStruct((tokens, B_QK_WIDTH), BF16),
            jax.ShapeDtypeStruct((tokens, B_QK_WIDTH), BF16),
            jax.ShapeDtypeStruct((tokens, B_V_WIDTH), BF16),
            jax.ShapeDtypeStruct((B_V_WIDTH, tokens), BF16),
            jax.ShapeDtypeStruct((tokens, B_V_WIDTH), BF16),
            jax.ShapeDtypeStruct((tokens, B_QK_WIDTH), F32),
            jax.ShapeDtypeStruct((tokens, B_QK_WIDTH), F32),
        ],
        compiler_params=_cparams(("parallel",)),
        name="b_in_proj",
    )(h, norm_gain, wmain, wz, wgf, wgb, bias_f, bias_b)


def _gla_block(q_ref, k_ref, v_ref, vt_ref, la_ref, tri_ref, st_ref, o_ref, scr, *, cb, reverse):
    qt_ref, kt_ref, ke_ref, dec_ref, kv_ref, sin_ref, att_ref = scr
    n_ch = cb // B_CHUNK
    pair_w = 2 * B_CHUNK
    row = lax.broadcasted_iota(jnp.int32, (B_CHUNK, B_CHUNK), 0)
    col = lax.broadcasted_iota(jnp.int32, (B_CHUNK, B_CHUNK), 1)
    amask = (col > row) if reverse else (col <= row)
    heads = [(slice(hh * B_KEY_DIM, (hh + 1) * B_KEY_DIM), slice(hh * B_VAL_DIM, (hh + 1) * B_VAL_DIM))
             for hh in range(B_HEADS)]
    chunk_rows = lambda c: slice(c * B_CHUNK, (c + 1) * B_CHUNK)

    for c in range(n_ch):
        rows = chunk_rows(c)
        g = la_ref[rows, :]
        g_hi = g.astype(BF16)
        g_lo = (g - g_hi.astype(F32)).astype(BF16)
        b = _dot(tri_ref[...], g_hi) + _dot(tri_ref[...], g_lo)
        dec = jnp.exp(b[0:1, :] if reverse else b[B_CHUNK - 1:B_CHUNK, :])
        dec_ref[c:c + 1, :] = dec
        qt_ref[rows, :] = (q_ref[rows, :].astype(F32) * jnp.exp(b)).astype(BF16)
        kt_f = k_ref[rows, :].astype(F32) * jnp.exp(-b)
        kt_ref[rows, :] = kt_f.astype(BF16)
        ke_ref[rows, :] = (kt_f * dec).astype(BF16)

    for c in range(n_ch):
        rows = chunk_rows(c)
        for hh, (sk, _) in enumerate(heads):
            att = jnp.where(amask, _nt_dot(qt_ref[rows, sk], kt_ref[rows, sk]), 0.0)
            att_ref[c, hh] = att.astype(BF16)

    zero = jnp.zeros((B_CHUNK, B_KEY_DIM), BF16)
    for pair in range(n_ch // 2):
        r0, r1 = chunk_rows(2 * pair), chunk_rows(2 * pair + 1)
        for hh, (sk, sv) in enumerate(heads):
            rhs = jnp.concatenate([jnp.concatenate([ke_ref[r0, sk], zero], axis=1),
                                   jnp.concatenate([zero, ke_ref[r1, sk]], axis=1)], axis=0)
            kv2 = _dot(vt_ref[sv, pair * pair_w:(pair + 1) * pair_w], rhs)
            kv_ref[2 * pair, hh] = kv2[:, :B_KEY_DIM]
            kv_ref[2 * pair + 1, hh] = kv2[:, B_KEY_DIM:]

    for step in range(n_ch):
        c = n_ch - 1 - step if reverse else step
        for hh, (sk, _) in enumerate(heads):
            st = st_ref[hh]
            sin_ref[c, hh] = st.astype(BF16)
            st_ref[hh] = st * dec_ref[c:c + 1, sk] + kv_ref[c, hh]

    for c in range(n_ch):
        rows = chunk_rows(c)
        for hh, (sk, sv) in enumerate(heads):
            o = _dot(att_ref[c, hh], v_ref[rows, sv]) + _nt_dot(qt_ref[rows, sk], sin_ref[c, hh])
            o_ref[rows, sv] = o.astype(o_ref.dtype)


def _gla_scratch(cb):
    n_ch = cb // B_CHUNK
    return [pltpu.VMEM((cb, B_QK_WIDTH), BF16), pltpu.VMEM((cb, B_QK_WIDTH), BF16),
            pltpu.VMEM((cb, B_QK_WIDTH), BF16), pltpu.VMEM((n_ch, B_QK_WIDTH), F32),
            pltpu.VMEM((n_ch, B_HEADS, B_VAL_DIM, B_KEY_DIM), F32),
            pltpu.VMEM((n_ch, B_HEADS, B_VAL_DIM, B_KEY_DIM), BF16),
            pltpu.VMEM((n_ch, B_HEADS, B_CHUNK, B_CHUNK), BF16)]


def _gla_fwd_kernel(q_ref, k_ref, v_ref, vt_ref, la_ref, tri_ref, o_ref, st_ref, *scr, cb):
    @pl.when(pl.program_id(1) == 0)
    def _():
        st_ref[...] = jnp.zeros_like(st_ref)

    _gla_block(q_ref, k_ref, v_ref, vt_ref, la_ref, tri_ref, st_ref, o_ref, scr, cb=cb, reverse=False)


def _gla_bwd_out_kernel(q_ref, k_ref, v_ref, vt_ref, la_ref, tri_ref, of_ref, sr_ref, og_ref, w_ref,
                        h_ref, out_ref, st_ref, ob_ref, *scr, cb):
    @pl.when(pl.program_id(1) == 0)
    def _():
        st_ref[...] = jnp.zeros_like(st_ref)

    _gla_block(q_ref, k_ref, v_ref, vt_ref, la_ref, tri_ref, st_ref, ob_ref, scr, cb=cb, reverse=True)
    ys = []
    for hh in range(B_HEADS):
        sv = slice(hh * B_VAL_DIM, (hh + 1) * B_VAL_DIM)
        o = of_ref[:, sv].astype(F32) + ob_ref[:, sv]
        ys.append((_rms_rows(o, og_ref[:, sv]) * sr_ref[:, sv].astype(F32)).astype(BF16))
    out_ref[...] = h_ref[...] + _dot(jnp.concatenate(ys, axis=1), w_ref[...])


def _gla_specs(batch, seq, cb, reverse):
    nb = seq // cb
    blk = (lambda b, i: b * nb + (nb - 1 - i)) if reverse else (lambda b, i: b * nb + i)
    row = lambda width: pl.BlockSpec((cb, width), lambda b, i: (blk(b, i), 0))
    const = lambda shape: pl.BlockSpec(shape, lambda b, i: (0,) * len(shape))
    vt = pl.BlockSpec((B_V_WIDTH, cb), lambda b, i: (0, blk(b, i)))
    return nb, row, const, vt


def _gla_fwd(q, k, v, vt, la, tri, *, batch, seq, cb=512):
    cb = min(cb, seq)
    nb, row, const, vt_spec = _gla_specs(batch, seq, cb, False)
    return pl.pallas_call(
        functools.partial(_gla_fwd_kernel, cb=cb),
        grid=(batch, nb),
        in_specs=[row(B_QK_WIDTH), row(B_QK_WIDTH), row(B_V_WIDTH), vt_spec, row(B_QK_WIDTH),
                  const((B_CHUNK, B_CHUNK))],
        out_specs=row(B_V_WIDTH),
        out_shape=jax.ShapeDtypeStruct((batch * seq, B_V_WIDTH), BF16),
        scratch_shapes=[pltpu.VMEM((B_HEADS, B_VAL_DIM, B_KEY_DIM), F32)] + _gla_scratch(cb),
        compiler_params=_cparams(("parallel", "arbitrary")),
        name="gla_fwd",
    )(q, k, v, vt, la, tri)


def _gla_bwd_out(q, k, v, vt, la, tri, o_f, silu_r, out_gain, w_out, h, *, batch, seq, cb=512):
    cb = min(cb, seq)
    nb, row, const, vt_spec = _gla_specs(batch, seq, cb, True)
    return pl.pallas_call(
        functools.partial(_gla_bwd_out_kernel, cb=cb),
        grid=(batch, nb),
        in_specs=[row(B_QK_WIDTH), row(B_QK_WIDTH), row(B_V_WIDTH), vt_spec, row(B_QK_WIDTH),
                  const((B_CHUNK, B_CHUNK)), row(B_V_WIDTH), row(B_V_WIDTH), const((1, B_V_WIDTH)),
                  const((B_V_WIDTH, D_MODEL)), row(D_MODEL)],
        out_specs=row(D_MODEL),
        out_shape=jax.ShapeDtypeStruct((batch * seq, D_MODEL), F32),
        scratch_shapes=[pltpu.VMEM((B_HEADS, B_VAL_DIM, B_KEY_DIM), F32),
                        pltpu.VMEM((cb, B_V_WIDTH), F32)] + _gla_scratch(cb),
        compiler_params=_cparams(("parallel", "arbitrary")),
        name="gla_bwd_out",
    )(q, k, v, vt, la, tri, o_f, silu_r, out_gain, w_out, h)


def _a_layer_params(w_in, q_gain, k_gain):
    n_g, n_slab, half = len(A_GROUPS), D_MODEL // SLAB, A_HEAD_DIM // 2
    w7 = w_in.reshape(D_MODEL, n_g, 3, n_slab, HEADS_PER_SLAB, 2, half)
    w_qk = w7[:, :, :2].transpose(0, 1, 2, 3, 5, 4, 6).reshape(D_MODEL, n_g, 2, D_MODEL)
    w_v = w7[:, :, 2:].reshape(D_MODEL, n_g, 1, D_MODEL)
    w_perm = jnp.concatenate([w_qk, w_v], axis=2).astype(BF16)
    w_perm = w_perm.transpose(1, 2, 0, 3).reshape(n_g * 3, D_MODEL, D_MODEL)

    def spread(gain):
        g5 = jnp.broadcast_to(gain.reshape(n_g, 1, 2, 1, half), (n_g, n_slab, 2, HEADS_PER_SLAB, half))
        return g5.reshape(n_g, 1, D_MODEL)

    gains = jnp.concatenate([spread(q_gain), spread(k_gain), jnp.ones((n_g, 1, D_MODEL), F32)], axis=1)
    return w_perm, gains.reshape(1, n_g * 3 * D_MODEL).astype(F32)


def _phase_major_index(dil):
    r = np.arange(PHASE_BLOCK)
    run = PHASE_BLOCK // dil
    return (r % run) * dil + r // run


def _rope_tables(seq):
    half = A_HEAD_DIM // 2
    inv_freq = ROPE_THETA ** (-jnp.arange(half, dtype=F32) / half)
    blocks = np.arange(seq // PHASE_BLOCK)[:, None] * PHASE_BLOCK
    pos = np.stack([(blocks + _phase_major_index(d)[None, :]).reshape(seq) for _, d in A_GROUPS])
    ang = jnp.asarray(pos, F32)[:, :, None] * inv_freq[None, None, :]
    reps = LANES // half
    return jnp.tile(jnp.cos(ang), (1, 1, reps)), jnp.tile(jnp.sin(ang), (1, 1, reps))


def _phase_perms():
    fwd = np.stack([np.eye(PHASE_BLOCK)[_phase_major_index(d)] for _, d in A_GROUPS[1:]])
    return jnp.asarray(fwd, BF16), jnp.asarray(fwd.transpose(0, 2, 1), BF16)


def _band_bias():
    r = np.arange(HEADS_PER_SLAB * Q_TILE)[:, None] % Q_TILE
    c = np.arange(K_TILE)[None, :]
    return jnp.asarray(np.where((c >= r) & (c <= r + 2 * A_HALF), 0.0, NEG_INF), F32)


def _segment_ones():
    a = np.arange(2 * LANES)
    return jnp.asarray((a[:, None] // 32) == (a[None, :] // 32), BF16)


def _head_expand():
    a = np.arange(2 * LANES) % LANES
    n = np.arange(D_MODEL) // A_HEAD_DIM
    return jnp.asarray(a[:, None] == n[None, :], BF16)


def _tri(reverse):
    r = np.arange(B_CHUNK)
    m = (r[None, :] >= r[:, None]) if reverse else (r[None, :] <= r[:, None])
    return jnp.asarray(m, BF16)


def _b_layer_params(w_in, w_gate_f, w_gate_b):
    n_main = 2 * B_QK_WIDTH + 2 * B_V_WIDTH
    wmain = w_in[:, :n_main].astype(BF16)
    wz = jnp.pad(w_in[:, n_main:], ((0, 0), (0, LANES - 2 * B_GATE_RANK))).astype(BF16)
    wgf = jnp.pad(w_gate_f, ((0, LANES - B_GATE_RANK), (0, 0))).astype(BF16)
    wgb = jnp.pad(w_gate_b, ((B_GATE_RANK, LANES - 2 * B_GATE_RANK), (0, 0))).astype(BF16)
    return wmain, wz, wgf, wgb


def _mixer_a(h, norm_gain, w_in, q_gain, k_gain, w_out, tables, *, batch, seq):
    cos_t, sin_t, seg, expand, perms, unperms, band = tables
    w_perm, qk_gain = _a_layer_params(w_in, q_gain, k_gain)
    qkv = _a_in_proj(h, norm_gain, w_perm, qk_gain, cos_t, sin_t, seg, perms, seq=seq)
    outs = [_attn_group(qkv, g, band, batch=batch, seq=seq) for g in range(len(A_GROUPS))]
    return _a_out_proj([o for o, _ in outs], [s for _, s in outs], expand, unperms, w_out.astype(BF16), h)


def _mixer_b(h, norm_gain, w_in, w_gate_f, bias_f, w_gate_b, bias_b, out_gain, w_out, *, batch, seq):
    wmain, wz, wgf, wgb = _b_layer_params(w_in, w_gate_f, w_gate_b)
    q, k, v, vt, silu_r, la_f, la_b = _b_in_proj(h, norm_gain, wmain, wz, wgf, wgb,
                                                bias_f[None, :], bias_b[None, :])
    o_f = _gla_fwd(q, k, v, vt, la_f, _tri(False), batch=batch, seq=seq)
    return _gla_bwd_out(q, k, v, vt, la_b, _tri(True), o_f, silu_r, out_gain.reshape(1, B_V_WIDTH),
                        w_out.astype(BF16), h, batch=batch, seq=seq)


def kernel(x, attn_norm, ffn_norm, a_w_in, a_q_norm, a_k_norm, a_w_out, b_w_in, b_w_gate_f, b_gate_bias_f,
           b_w_gate_b, b_gate_bias_b, b_out_norm, b_w_out, ffn_w_gate_up, ffn_w_down):
    batch, seq, _ = x.shape
    h = x.reshape(batch * seq, D_MODEL)
    tables = _rope_tables(seq) + (_segment_ones(), _head_expand()) + _phase_perms() + (_band_bias(),)
    for i in range(DEPTH):
        j = i // 2
        if i % 2 == 0:
            h = _mixer_a(h, attn_norm[i][None, :], a_w_in[j], a_q_norm[j], a_k_norm[j], a_w_out[j], tables,
                         batch=batch, seq=seq)
        else:
            h = _mixer_b(h, attn_norm[i][None, :], b_w_in[j], b_w_gate_f[j], b_gate_bias_f[j], b_w_gate_b[j],
                         b_gate_bias_b[j], b_out_norm[j], b_w_out[j], batch=batch, seq=seq)
        h = _ffn(h, ffn_norm[i][None, :], ffn_w_gate_up[i].astype(BF16), ffn_w_down[i].astype(BF16))
    return h.reshape(batch, seq, D_MODEL)
```

```python
import functools

import numpy as np
import jax
import jax.numpy as jnp
from jax import lax
from jax.experimental import pallas as pl
from jax.experimental.pallas import tpu as pltpu

F32 = jnp.float32
BF16 = jnp.bfloat16

D_MODEL = 1024
DEPTH = 4
RMS_EPS = 1e-6
NEG_INF = -1e30

A_GROUPS = ((128, 1), (512, 4), (2048, 16))
A_HEADS = 16
A_HEAD_DIM = 64
A_HALF = 64
ROPE_THETA = 10000.0
LOG2E = 1.4426950408889634
LN2 = 0.6931471805599453
Q_SCALE = A_HEAD_DIM ** -0.5 * LOG2E

B_HEADS = 4
B_KEY_DIM = 128
B_VAL_DIM = 256
B_QK_WIDTH = B_HEADS * B_KEY_DIM
B_V_WIDTH = B_HEADS * B_VAL_DIM
B_GATE_RANK = 16
B_GATE_TAU = 16.0
B_CHUNK = 64

FFN_HIDDEN = 2816

LANES = 128
V7X_VMEM_BYTES = 64 * 1024 * 1024
VMEM_LIMIT_BYTES = V7X_VMEM_BYTES - 8 * 1024 * 1024

Q_TILE = 128
K_TILE = Q_TILE + 2 * A_HALF
HEADS_PER_SLAB = 4
SLAB = HEADS_PER_SLAB * A_HEAD_DIM
PHASE_BLOCK = 256

for _w, _d in A_GROUPS:
    assert _w // (2 * _d) == A_HALF


def _cparams(sem):
    return pltpu.CompilerParams(dimension_semantics=sem, vmem_limit_bytes=VMEM_LIMIT_BYTES)


def _nt_dot(a, b):
    return lax.dot_general(a, b, (((1,), (1,)), ((), ())), preferred_element_type=F32)


def _dot(a, b):
    return jnp.dot(a, b, preferred_element_type=F32)


def _rms_rows(x, gain):
    ms = jnp.mean(x * x, axis=-1, keepdims=True)
    return x * lax.rsqrt(ms + RMS_EPS) * gain


def _a_in_kernel(h_ref, ng_ref, w_ref, qkg_ref, cos_ref, sin_ref, seg_ref, perm_ref, out_ref, *, tm):
    g = pl.program_id(1)
    sub = PHASE_BLOCK
    perm = perm_ref[g]
    for s in range(tm // sub):
        rows = slice(s * sub, (s + 1) * sub)
        xn = _rms_rows(h_ref[rows, :], ng_ref[...]).astype(BF16)
        x = _dot(perm, xn).astype(BF16)
        cos = cos_ref[rows, :]
        sin = sin_ref[rows, :]
        for chunk, scale in ((0, Q_SCALE), (1, 1.0)):
            acc = _dot(x, w_ref[:, chunk * D_MODEL:(chunk + 1) * D_MODEL])
            base = chunk * D_MODEL
            for pair in range(D_MODEL // (2 * SLAB)):
                ab = []
                for g4 in (2 * pair, 2 * pair + 1):
                    c0 = g4 * SLAB
                    ab.append((acc[:, c0:c0 + LANES], acc[:, c0 + LANES:c0 + SLAB]))
                sq = jnp.concatenate([a * a + b * b for a, b in ab], axis=1).astype(BF16)
                ss = _dot(sq, seg_ref[...])
                for k, (a, b) in enumerate(ab):
                    c0 = base + (2 * pair + k) * SLAB
                    r = lax.rsqrt(ss[:, k * LANES:(k + 1) * LANES] * (1.0 / A_HEAD_DIM) + RMS_EPS) * scale
                    an = a * r * qkg_ref[:, c0:c0 + LANES]
                    bn = b * r * qkg_ref[:, c0 + LANES:c0 + SLAB]
                    out_ref[rows, c0:c0 + LANES] = (an * cos - bn * sin).astype(BF16)
                    out_ref[rows, c0 + LANES:c0 + SLAB] = (bn * cos + an * sin).astype(BF16)
        out_ref[rows, 2 * D_MODEL:] = _dot(x, w_ref[:, 2 * D_MODEL:]).astype(BF16)


def _a_in_proj(h, norm_gain, w_perm, qk_gain, cos_t, sin_t, seg, perms, *, seq, tm=1024):
    tokens = h.shape[0]
    n_groups = w_perm.shape[1] // (3 * D_MODEL)
    tm = min(tm, seq)
    pos_blocks = seq // tm
    table = pl.BlockSpec((None, tm, LANES), lambda i, g: (g, i % pos_blocks, 0))
    return pl.pallas_call(
        functools.partial(_a_in_kernel, tm=tm),
        grid=(tokens // tm, n_groups),
        in_specs=[
            pl.BlockSpec((tm, D_MODEL), lambda i, g: (i, 0)),
            pl.BlockSpec((1, D_MODEL), lambda i, g: (0, 0)),
            pl.BlockSpec((D_MODEL, 3 * D_MODEL), lambda i, g: (0, g)),
            pl.BlockSpec((1, 3 * D_MODEL), lambda i, g: (0, g)),
            table,
            table,
            pl.BlockSpec(seg.shape, lambda i, g: (0, 0)),
            pl.BlockSpec(perms.shape, lambda i, g: (0, 0, 0)),
        ],
        out_specs=pl.BlockSpec((tm, 3 * D_MODEL), lambda i, g: (i, g)),
        out_shape=jax.ShapeDtypeStruct((tokens, n_groups * 3 * D_MODEL), BF16),
        compiler_params=_cparams(("parallel", "parallel")),
        name="a_in_proj",
    )(h, norm_gain, w_perm, qk_gain, cos_t, sin_t, seg, perms)


def _attn_kernel(q_ref, kp_ref, kc_ref, kn_ref, vp_ref, vc_ref, vn_ref, band_ref, o_ref, st_ref,
                 q2_ref, kcat_ref, vm_ref, o2_ref, st2_ref, *, tq, phase_len):
    i = pl.program_id(2)
    flat = lambda ref: ref[...].reshape(-1, ref.shape[-1])
    q2_ref[...] = flat(q_ref)
    kcat_ref[0:A_HALF, :] = flat(kp_ref)
    kcat_ref[A_HALF:A_HALF + tq, :] = flat(kc_ref)
    kcat_ref[A_HALF + tq:2 * A_HALF + tq, :] = flat(kn_ref)
    lane = lax.broadcasted_iota(jnp.int32, (1, D_MODEL), 1)
    for hi in range(HEADS_PER_SLAB):
        keep = (((lane & (SLAB - 1)) >> 6) == hi).astype(F32).astype(BF16)
        vm_ref[hi, 0:A_HALF, :] = flat(vp_ref) * keep
        vm_ref[hi, A_HALF:A_HALF + tq, :] = flat(vc_ref) * keep
        vm_ref[hi, A_HALF + tq:2 * A_HALF + tq, :] = flat(vn_ref) * keep

    stack = HEADS_PER_SLAB * Q_TILE
    key_col = lax.broadcasted_iota(jnp.int32, (1, K_TILE), 1)
    col1 = lax.broadcasted_iota(jnp.int32, (Q_TILE, K_TILE), 1)
    q_head = (col1 & (LANES - 1)) >> 5
    q_keep = [(q_head == hi).astype(F32).astype(BF16) for hi in range(HEADS_PER_SLAB)]
    st_col = lax.broadcasted_iota(jnp.int32, (Q_TILE, LANES), 1)

    def tile(s):
        r0 = s * Q_TILE
        key_pos = i * tq + s * Q_TILE - A_HALF + key_col
        in_seq = (key_pos >= 0) & (key_pos < phase_len)
        bias = band_ref[...] + jnp.where(in_seq, 0.0, NEG_INF)
        stats = jnp.zeros((Q_TILE, LANES), F32)
        for hg in range(D_MODEL // SLAB):
            cs = slice(hg * SLAB, (hg + 1) * SLAB)
            q4 = q2_ref[pl.ds(r0, Q_TILE), cs]
            k4 = kcat_ref[pl.ds(r0, K_TILE), cs]
            qm = jnp.concatenate([q4 * q_keep[hi] for hi in range(HEADS_PER_SLAB)], axis=0)
            sc = _nt_dot(qm, k4) + bias
            m = jnp.max(sc, axis=-1, keepdims=True)
            p = jnp.exp2(sc - m)
            l = jnp.sum(p, axis=-1, keepdims=True)
            pb = p.astype(BF16)
            m_rep = jnp.broadcast_to(m, (stack, LANES))
            l_rep = jnp.broadcast_to(l, (stack, LANES))
            o4 = None
            for hi in range(HEADS_PER_SLAB):
                rows = slice(hi * Q_TILE, (hi + 1) * Q_TILE)
                part = _dot(pb[rows], vm_ref[hi, pl.ds(r0, K_TILE), cs])
                o4 = part if o4 is None else o4 + part
                head = hg * HEADS_PER_SLAB + hi
                stats = jnp.where(st_col == head, m_rep[rows], stats)
                stats = jnp.where(st_col == A_HEADS + head, l_rep[rows], stats)
            o2_ref[pl.ds(r0, Q_TILE), cs] = o4.astype(BF16)
        st2_ref[pl.ds(r0, Q_TILE), :] = stats

    for s in range(tq // Q_TILE):
        tile(s)
    o_ref[...] = o2_ref[...].reshape(o_ref.shape)
    st_ref[...] = st2_ref[...].reshape(st_ref.shape)


def _attn_group(qkv, group, band, *, batch, seq, tq=512):
    dil = A_GROUPS[group][1]
    tokens, width = qkv.shape
    phase_len = seq // dil
    tq = min(tq, phase_len)
    nq = phase_len // tq
    hb = phase_len // A_HALF
    r = tq // A_HALF
    run = PHASE_BLOCK // dil
    cq = group * 3
    cur_i = lambda b, i: b * nq + i
    prev_i = lambda b, i: b * hb + jnp.maximum(i * r - 1, 0)
    next_i = lambda b, i: b * hb + jnp.minimum((i + 1) * r, hb - 1)

    if dil == 1:
        view_shape = lambda cols: (tokens, cols)
        spec = lambda rows, cols, row_i, col_i: pl.BlockSpec(
            (rows, cols), lambda b, p, i: (row_i(b, i), col_i))
    else:
        view_shape = lambda cols: (tokens // PHASE_BLOCK, dil, run, cols)
        spec = lambda rows, cols, row_i, col_i: pl.BlockSpec(
            (rows // run, None, run, cols), lambda b, p, i: (row_i(b, i), p, 0, col_i))
    cur = lambda c: spec(tq, D_MODEL, cur_i, c)
    prev = lambda c: spec(A_HALF, D_MODEL, prev_i, c)
    nxt = lambda c: spec(A_HALF, D_MODEL, next_i, c)

    qkv_v = qkv.reshape(view_shape(width))
    o, stats = pl.pallas_call(
        functools.partial(_attn_kernel, tq=tq, phase_len=phase_len),
        grid=(batch, dil, nq),
        in_specs=[cur(cq), prev(cq + 1), cur(cq + 1), nxt(cq + 1), prev(cq + 2), cur(cq + 2), nxt(cq + 2),
                  pl.BlockSpec(band.shape, lambda b, p, i: (0, 0))],
        out_specs=[spec(tq, D_MODEL, cur_i, 0), spec(tq, LANES, cur_i, 0)],
        out_shape=[
            jax.ShapeDtypeStruct(view_shape(D_MODEL), BF16),
            jax.ShapeDtypeStruct(view_shape(LANES), F32),
        ],
        scratch_shapes=[pltpu.VMEM((tq, D_MODEL), BF16),
                        pltpu.VMEM((tq + 2 * A_HALF, D_MODEL), BF16),
                        pltpu.VMEM((HEADS_PER_SLAB, tq + 2 * A_HALF, D_MODEL), BF16),
                        pltpu.VMEM((tq, D_MODEL), BF16),
                        pltpu.VMEM((tq, LANES), F32)],
        compiler_params=_cparams(("parallel", "parallel", "parallel")),
        name=f"attn_group{group}",
    )(qkv_v, qkv_v, qkv_v, qkv_v, qkv_v, qkv_v, qkv_v, band)
    return o.reshape(tokens, D_MODEL), stats.reshape(tokens, LANES)


def _a_out_kernel(o0_ref, o1_ref, o2_ref, s0_ref, s1_ref, s2_ref, exp_ref, unperm_ref, w_ref, h_ref, out_ref,
                  *, tm):
    for s in range(tm // PHASE_BLOCK):
        rows = slice(s * PHASE_BLOCK, (s + 1) * PHASE_BLOCK)
        outs = [o0_ref[rows, :].astype(F32)]
        stats = [s0_ref[rows, :]]
        for g, (o_ref, s_ref) in enumerate(((o1_ref, s1_ref), (o2_ref, s2_ref))):
            outs.append(_dot(unperm_ref[g], o_ref[rows, :]))
            st = s_ref[rows, :]
            hi = st.astype(BF16)
            lo = (st - hi.astype(F32)).astype(BF16)
            both = _dot(unperm_ref[g], jnp.concatenate([hi, lo], axis=1))
            stats.append(both[:, :LANES] + both[:, LANES:])
        is_head = lax.broadcasted_iota(jnp.int32, (PHASE_BLOCK, LANES), 1) < A_HEADS
        dens = [jnp.where(is_head, pltpu.roll(st, LANES - A_HEADS, 1), 1.0) for st in stats]
        lses = [jnp.where(is_head, st * LN2 + jnp.log(d), 0.0) for st, d in zip(stats, dens)]
        m = jnp.maximum(jnp.maximum(lses[0], lses[1]), lses[2])
        es = [jnp.exp(x - m) for x in lses]
        inv = 1.0 / (es[0] + es[1] + es[2])
        mix = None
        for e, d, o in zip(es, dens, outs):
            wgt = jnp.where(is_head, e * inv / d, 0.0)
            hi = wgt.astype(BF16)
            lo = (wgt - hi.astype(F32)).astype(BF16)
            wide = _dot(jnp.concatenate([hi, lo], axis=1), exp_ref[...])
            mix = wide * o if mix is None else mix + wide * o
        out_ref[rows, :] = h_ref[rows, :] + _dot(mix.astype(BF16), w_ref[...])


def _a_out_proj(os_, stats, expand, unperms, w_out, h, *, tm=1024):
    tokens = h.shape[0]
    row = lambda i: (i, 0)
    const = lambda i: (0, 0)
    return pl.pallas_call(
        functools.partial(_a_out_kernel, tm=tm),
        grid=(tokens // tm,),
        in_specs=[pl.BlockSpec((tm, D_MODEL), row)] * 3 + [pl.BlockSpec((tm, LANES), row)] * 3 + [
            pl.BlockSpec((2 * LANES, D_MODEL), const),
            pl.BlockSpec(unperms.shape, lambda i: (0, 0, 0)),
            pl.BlockSpec((D_MODEL, D_MODEL), const),
            pl.BlockSpec((tm, D_MODEL), row),
        ],
        out_specs=pl.BlockSpec((tm, D_MODEL), row),
        out_shape=jax.ShapeDtypeStruct((tokens, D_MODEL), F32),
        compiler_params=_cparams(("parallel",)),
        name="a_out_proj",
    )(*os_, *stats, expand, unperms, w_out, h)


def _ffn_kernel(h_ref, g_ref, wgu_ref, wd_ref, out_ref, *, th):
    x = h_ref[...]
    xn = _rms_rows(x, g_ref[...]).astype(BF16)
    acc = x
    for c in range(FFN_HIDDEN // th):
        cols = slice(c * th, (c + 1) * th)
        gate = _dot(xn, wgu_ref[:, cols])
        up = _dot(xn, wgu_ref[:, FFN_HIDDEN + c * th:FFN_HIDDEN + (c + 1) * th])
        act = (gate * jax.nn.sigmoid(gate) * up).astype(BF16)
        acc = acc + _dot(act, wd_ref[cols, :])
    out_ref[...] = acc


def _ffn(h, gain, wgu, wd, layer, *, tm=512, th=256):
    tokens = h.shape[0]
    return pl.pallas_call(
        functools.partial(_ffn_kernel, th=th),
        grid=(tokens // tm,),
        in_specs=[
            pl.BlockSpec((tm, D_MODEL), lambda i: (i, 0)),
            pl.BlockSpec((1, D_MODEL), lambda i: (0, 0)),
            pl.BlockSpec((None,) + wgu.shape[1:], lambda i: (layer, 0, 0)),
            pl.BlockSpec((None,) + wd.shape[1:], lambda i: (layer, 0, 0)),
        ],
        out_specs=pl.BlockSpec((tm, D_MODEL), lambda i: (i, 0)),
        out_shape=jax.ShapeDtypeStruct((tokens, D_MODEL), F32),
        compiler_params=_cparams(("parallel",)),
        name="ffn",
    )(h, gain, wgu, wd)


def _log_sigmoid(x):
    return jnp.minimum(x, 0.0) - jnp.log1p(jnp.exp(-jnp.abs(x)))


def _b_in_kernel(h_ref, ng_ref, wmain_ref, wz_ref, wgf_ref, wgb_ref, bf_ref, bb_ref,
                 q_ref, k_ref, v_ref, vt_ref, sr_ref, laf_ref, lab_ref, *, tm):
    sub = PHASE_BLOCK
    for s in range(tm // sub):
        rows = slice(s * sub, (s + 1) * sub)
        xn = _rms_rows(h_ref[rows, :], ng_ref[...]).astype(BF16)
        y = _dot(xn, wmain_ref[...])
        q_ref[rows, :] = (y[:, :B_QK_WIDTH] * (B_KEY_DIM ** -0.5)).astype(BF16)
        k_ref[rows, :] = y[:, B_QK_WIDTH:2 * B_QK_WIDTH].astype(BF16)
        v = y[:, 2 * B_QK_WIDTH:2 * B_QK_WIDTH + B_V_WIDTH]
        v_ref[rows, :] = v.astype(BF16)
        vt_ref[:, rows] = v.T.astype(BF16)
        r = y[:, 2 * B_QK_WIDTH + B_V_WIDTH:]
        sr_ref[rows, :] = (r * jax.nn.sigmoid(r)).astype(BF16)
        z = _dot(xn, wz_ref[...]).astype(BF16)
        laf_ref[rows, :] = _log_sigmoid(_dot(z, wgf_ref[...]) + bf_ref[...]) * (1.0 / B_GATE_TAU)
        lab_ref[rows, :] = _log_sigmoid(_dot(z, wgb_ref[...]) + bb_ref[...]) * (1.0 / B_GATE_TAU)


def _b_in_proj(h, norm_gain, w_all, layer, wz, wgf, wgb, bias_f, bias_b, *, tm=1024):
    tokens = h.shape[0]
    row = lambda i: (i, 0)
    const = lambda i: (0, 0)
    full = lambda a: pl.BlockSpec(a.shape, const)
    n_main = 2 * B_QK_WIDTH + 2 * B_V_WIDTH
    return pl.pallas_call(
        functools.partial(_b_in_kernel, tm=tm),
        grid=(tokens // tm,),
        in_specs=[pl.BlockSpec((tm, D_MODEL), row), full(norm_gain),
                  pl.BlockSpec((None, D_MODEL, n_main), lambda i: (layer, 0, 0))] + [full(a) for a in
                  (wz, wgf, wgb, bias_f, bias_b)],
        out_specs=[
            pl.BlockSpec((tm, B_QK_WIDTH), row),
            pl.BlockSpec((tm, B_QK_WIDTH), row),
            pl.BlockSpec((tm, B_V_WIDTH), row),
            pl.BlockSpec((B_V_WIDTH, tm), lambda i: (0, i)),
            pl.BlockSpec((tm, B_V_WIDTH), row),
            pl.BlockSpec((tm, B_QK_WIDTH), row),
            pl.BlockSpec((tm, B_QK_WIDTH), row),
        ],
        out_shape=[
            jax.ShapeDtypeStruct((tokens, B_QK_WIDTH), BF16),
            jax.ShapeDtypeStruct((tokens, B_QK_WIDTH), BF16),
            jax.ShapeDtypeStruct((tokens, B_V_WIDTH), BF16),
            jax.ShapeDtypeStruct((B_V_WIDTH, tokens), BF16),
            jax.ShapeDtypeStruct((tokens, B_V_WIDTH), BF16),
            jax.ShapeDtypeStruct((tokens, B_QK_WIDTH), F32),
            jax.ShapeDtypeStruct((tokens, B_QK_WIDTH), F32),
        ],
        compiler_params=_cparams(("parallel",)),
        name="b_in_proj",
    )(h, norm_gain, w_all, wz, wgf, wgb, bias_f, bias_b)


def _gla_block(q_ref, k_ref, v_ref, vt_ref, la_ref, tri_ref, st_ref, o_ref, scr, *, cb, reverse):
    qt_ref, kt_ref, ke_ref, dec_ref, kv_ref, sin_ref, att_ref = scr
    n_ch = cb // B_CHUNK
    pair_w = 2 * B_CHUNK
    row = lax.broadcasted_iota(jnp.int32, (B_CHUNK, B_CHUNK), 0)
    col = lax.broadcasted_iota(jnp.int32, (B_CHUNK, B_CHUNK), 1)
    amask = (col > row) if reverse else (col <= row)
    heads = [(slice(hh * B_KEY_DIM, (hh + 1) * B_KEY_DIM), slice(hh * B_VAL_DIM, (hh + 1) * B_VAL_DIM))
             for hh in range(B_HEADS)]
    chunk_rows = lambda c: slice(c * B_CHUNK, (c + 1) * B_CHUNK)

    for c in range(n_ch):
        rows = chunk_rows(c)
        g = la_ref[rows, :]
        g_hi = g.astype(BF16)
        g_lo = (g - g_hi.astype(F32)).astype(BF16)
        b = _dot(tri_ref[...], g_hi) + _dot(tri_ref[...], g_lo)
        dec = jnp.exp(b[0:1, :] if reverse else b[B_CHUNK - 1:B_CHUNK, :])
        dec_ref[c:c + 1, :] = dec
        qt_ref[rows, :] = (q_ref[rows, :].astype(F32) * jnp.exp(b)).astype(BF16)
        kt_f = k_ref[rows, :].astype(F32) * jnp.exp(-b)
        kt_ref[rows, :] = kt_f.astype(BF16)
        ke_ref[rows, :] = (kt_f * dec).astype(BF16)

    for c in range(n_ch):
        rows = chunk_rows(c)
        for hh, (sk, _) in enumerate(heads):
            att = jnp.where(amask, _nt_dot(qt_ref[rows, sk], kt_ref[rows, sk]), 0.0)
            att_ref[c, hh] = att.astype(BF16)

    zero = jnp.zeros((B_CHUNK, B_KEY_DIM), BF16)
    for pair in range(n_ch // 2):
        r0, r1 = chunk_rows(2 * pair), chunk_rows(2 * pair + 1)
        for hh, (sk, sv) in enumerate(heads):
            rhs = jnp.concatenate([jnp.concatenate([ke_ref[r0, sk], zero], axis=1),
                                   jnp.concatenate([zero, ke_ref[r1, sk]], axis=1)], axis=0)
            kv2 = _dot(vt_ref[sv, pair * pair_w:(pair + 1) * pair_w], rhs)
            kv_ref[2 * pair, hh] = kv2[:, :B_KEY_DIM]
            kv_ref[2 * pair + 1, hh] = kv2[:, B_KEY_DIM:]

    for step in range(n_ch):
        c = n_ch - 1 - step if reverse else step
        for hh, (sk, _) in enumerate(heads):
            st = st_ref[hh]
            sin_ref[c, hh] = st.astype(BF16)
            st_ref[hh] = st * dec_ref[c:c + 1, sk] + kv_ref[c, hh]

    for c in range(n_ch):
        rows = chunk_rows(c)
        for hh, (sk, sv) in enumerate(heads):
            o = _dot(att_ref[c, hh], v_ref[rows, sv]) + _nt_dot(qt_ref[rows, sk], sin_ref[c, hh])
            o_ref[rows, sv] = o.astype(o_ref.dtype)


def _gla_scratch(cb):
    n_ch = cb // B_CHUNK
    return [pltpu.VMEM((cb, B_QK_WIDTH), BF16), pltpu.VMEM((cb, B_QK_WIDTH), BF16),
            pltpu.VMEM((cb, B_QK_WIDTH), BF16), pltpu.VMEM((n_ch, B_QK_WIDTH), F32),
            pltpu.VMEM((n_ch, B_HEADS, B_VAL_DIM, B_KEY_DIM), F32),
            pltpu.VMEM((n_ch, B_HEADS, B_VAL_DIM, B_KEY_DIM), BF16),
            pltpu.VMEM((n_ch, B_HEADS, B_CHUNK, B_CHUNK), BF16)]


def _gla_fwd_kernel(q_ref, k_ref, v_ref, vt_ref, la_ref, tri_ref, o_ref, st_ref, *scr, cb):
    @pl.when(pl.program_id(1) == 0)
    def _():
        st_ref[...] = jnp.zeros_like(st_ref)

    _gla_block(q_ref, k_ref, v_ref, vt_ref, la_ref, tri_ref, st_ref, o_ref, scr, cb=cb, reverse=False)


def _gla_bwd_out_kernel(q_ref, k_ref, v_ref, vt_ref, la_ref, tri_ref, of_ref, sr_ref, og_ref, w_ref,
                        h_ref, out_ref, st_ref, ob_ref, *scr, cb):
    @pl.when(pl.program_id(1) == 0)
    def _():
        st_ref[...] = jnp.zeros_like(st_ref)

    _gla_block(q_ref, k_ref, v_ref, vt_ref, la_ref, tri_ref, st_ref, ob_ref, scr, cb=cb, reverse=True)
    ys = []
    for hh in range(B_HEADS):
        sv = slice(hh * B_VAL_DIM, (hh + 1) * B_VAL_DIM)
        o = of_ref[:, sv].astype(F32) + ob_ref[:, sv]
        ys.append((_rms_rows(o, og_ref[:, sv]) * sr_ref[:, sv].astype(F32)).astype(BF16))
    out_ref[...] = h_ref[...] + _dot(jnp.concatenate(ys, axis=1), w_ref[...])


def _gla_specs(batch, seq, cb, reverse):
    nb = seq // cb
    blk = (lambda b, i: b * nb + (nb - 1 - i)) if reverse else (lambda b, i: b * nb + i)
    row = lambda width: pl.BlockSpec((cb, width), lambda b, i: (blk(b, i), 0))
    const = lambda shape: pl.BlockSpec(shape, lambda b, i: (0,) * len(shape))
    vt = pl.BlockSpec((B_V_WIDTH, cb), lambda b, i: (0, blk(b, i)))
    return nb, row, const, vt


def _gla_fwd(q, k, v, vt, la, tri, *, batch, seq, cb=512):
    cb = min(cb, seq)
    nb, row, const, vt_spec = _gla_specs(batch, seq, cb, False)
    return pl.pallas_call(
        functools.partial(_gla_fwd_kernel, cb=cb),
        grid=(batch, nb),
        in_specs=[row(B_QK_WIDTH), row(B_QK_WIDTH), row(B_V_WIDTH), vt_spec, row(B_QK_WIDTH),
                  const((B_CHUNK, B_CHUNK))],
        out_specs=row(B_V_WIDTH),
        out_shape=jax.ShapeDtypeStruct((batch * seq, B_V_WIDTH), BF16),
        scratch_shapes=[pltpu.VMEM((B_HEADS, B_VAL_DIM, B_KEY_DIM), F32)] + _gla_scratch(cb),
        compiler_params=_cparams(("parallel", "arbitrary")),
        name="gla_fwd",
    )(q, k, v, vt, la, tri)


def _gla_bwd_out(q, k, v, vt, la, tri, o_f, silu_r, out_gain, w_out, h, *, batch, seq, cb=512):
    cb = min(cb, seq)
    nb, row, const, vt_spec = _gla_specs(batch, seq, cb, True)
    return pl.pallas_call(
        functools.partial(_gla_bwd_out_kernel, cb=cb),
        grid=(batch, nb),
        in_specs=[row(B_QK_WIDTH), row(B_QK_WIDTH), row(B_V_WIDTH), vt_spec, row(B_QK_WIDTH),
                  const((B_CHUNK, B_CHUNK)), row(B_V_WIDTH), row(B_V_WIDTH), const((1, B_V_WIDTH)),
                  const((B_V_WIDTH, D_MODEL)), row(D_MODEL)],
        out_specs=row(D_MODEL),
        out_shape=jax.ShapeDtypeStruct((batch * seq, D_MODEL), F32),
        scratch_shapes=[pltpu.VMEM((B_HEADS, B_VAL_DIM, B_KEY_DIM), F32),
                        pltpu.VMEM((cb, B_V_WIDTH), F32)] + _gla_scratch(cb),
        compiler_params=_cparams(("parallel", "arbitrary")),
        name="gla_bwd_out",
    )(q, k, v, vt, la, tri, o_f, silu_r, out_gain, w_out, h)


def _a_layer_params(w_in, q_gain, k_gain):
    n_g, n_slab, half = len(A_GROUPS), D_MODEL // SLAB, A_HEAD_DIM // 2
    w7 = w_in.reshape(D_MODEL, n_g, 3, n_slab, HEADS_PER_SLAB, 2, half)
    w_qk = w7[:, :, :2].transpose(0, 1, 2, 3, 5, 4, 6).reshape(D_MODEL, n_g, 2, D_MODEL)
    w_v = w7[:, :, 2:].reshape(D_MODEL, n_g, 1, D_MODEL)
    w_perm = jnp.concatenate([w_qk, w_v], axis=2).reshape(D_MODEL, n_g * 3 * D_MODEL).astype(BF16)

    def spread(gain):
        g5 = jnp.broadcast_to(gain.reshape(n_g, 1, 2, 1, half), (n_g, n_slab, 2, HEADS_PER_SLAB, half))
        return g5.reshape(n_g, 1, D_MODEL)

    gains = jnp.concatenate([spread(q_gain), spread(k_gain), jnp.ones((n_g, 1, D_MODEL), F32)], axis=1)
    return w_perm, gains.reshape(1, n_g * 3 * D_MODEL).astype(F32)


def _phase_major_index(dil):
    r = np.arange(PHASE_BLOCK)
    run = PHASE_BLOCK // dil
    return (r % run) * dil + r // run


def _rope_tables(seq):
    half = A_HEAD_DIM // 2
    inv_freq = ROPE_THETA ** (-jnp.arange(half, dtype=F32) / half)
    blocks = np.arange(seq // PHASE_BLOCK)[:, None] * PHASE_BLOCK
    pos = np.stack([(blocks + _phase_major_index(d)[None, :]).reshape(seq) for _, d in A_GROUPS])
    ang = jnp.asarray(pos, F32)[:, :, None] * inv_freq[None, None, :]
    reps = LANES // half
    return jnp.tile(jnp.cos(ang), (1, 1, reps)), jnp.tile(jnp.sin(ang), (1, 1, reps))


def _phase_perms():
    fwd = np.stack([np.eye(PHASE_BLOCK)[_phase_major_index(d)] for _, d in A_GROUPS])
    return jnp.asarray(fwd, BF16), jnp.asarray(fwd[1:].transpose(0, 2, 1), BF16)


def _band_bias():
    r = np.arange(HEADS_PER_SLAB * Q_TILE)[:, None] % Q_TILE
    c = np.arange(K_TILE)[None, :]
    return jnp.asarray(np.where((c >= r) & (c <= r + 2 * A_HALF), 0.0, NEG_INF), F32)


def _segment_ones():
    a = np.arange(2 * LANES)
    return jnp.asarray((a[:, None] // 32) == (a[None, :] // 32), BF16)


def _head_expand():
    a = np.arange(2 * LANES) % LANES
    n = np.arange(D_MODEL) // A_HEAD_DIM
    return jnp.asarray(a[:, None] == n[None, :], BF16)


def _tri(reverse):
    r = np.arange(B_CHUNK)
    m = (r[None, :] >= r[:, None]) if reverse else (r[None, :] <= r[:, None])
    return jnp.asarray(m, BF16)


def _b_layer_params(w_in, w_gate_f, w_gate_b):
    n_main = 2 * B_QK_WIDTH + 2 * B_V_WIDTH
    wz = jnp.pad(w_in[:, n_main:], ((0, 0), (0, LANES - 2 * B_GATE_RANK))).astype(BF16)
    wgf = jnp.pad(w_gate_f, ((0, LANES - B_GATE_RANK), (0, 0))).astype(BF16)
    wgb = jnp.pad(w_gate_b, ((B_GATE_RANK, LANES - 2 * B_GATE_RANK), (0, 0))).astype(BF16)
    return wz, wgf, wgb


def _mixer_a(h, norm_gain, w_in, q_gain, k_gain, w_out, tables, *, batch, seq):
    cos_t, sin_t, seg, expand, perms, unperms, band = tables
    w_perm, qk_gain = _a_layer_params(w_in, q_gain, k_gain)
    qkv = _a_in_proj(h, norm_gain, w_perm, qk_gain, cos_t, sin_t, seg, perms, seq=seq)
    outs = [_attn_group(qkv, g, band, batch=batch, seq=seq) for g in range(len(A_GROUPS))]
    return _a_out_proj([o for o, _ in outs], [s for _, s in outs], expand, unperms, w_out.astype(BF16), h)


def _mixer_b(h, norm_gain, w_all, layer, w_in, w_gate_f, bias_f, w_gate_b, bias_b, out_gain, w_out, *, batch, seq):
    wz, wgf, wgb = _b_layer_params(w_in, w_gate_f, w_gate_b)
    q, k, v, vt, silu_r, la_f, la_b = _b_in_proj(h, norm_gain, w_all, layer, wz, wgf, wgb,
                                                bias_f[None, :], bias_b[None, :])
    o_f = _gla_fwd(q, k, v, vt, la_f, _tri(False), batch=batch, seq=seq)
    return _gla_bwd_out(q, k, v, vt, la_b, _tri(True), o_f, silu_r, out_gain.reshape(1, B_V_WIDTH),
                        w_out.astype(BF16), h, batch=batch, seq=seq)


def kernel(x, attn_norm, ffn_norm, a_w_in, a_q_norm, a_k_norm, a_w_out, b_w_in, b_w_gate_f, b_gate_bias_f,
           b_w_gate_b, b_gate_bias_b, b_out_norm, b_w_out, ffn_w_gate_up, ffn_w_down):
    batch, seq, _ = x.shape
    h = x.reshape(batch * seq, D_MODEL)
    ffn_wgu, ffn_wd, b_w = ffn_w_gate_up.astype(BF16), ffn_w_down.astype(BF16), b_w_in.astype(BF16)
    tables = _rope_tables(seq) + (_segment_ones(), _head_expand()) + _phase_perms() + (_band_bias(),)
    for i in range(DEPTH):
        j = i // 2
        if i % 2 == 0:
            h = _mixer_a(h, attn_norm[i][None, :], a_w_in[j], a_q_norm[j], a_k_norm[j], a_w_out[j], tables,
                         batch=batch, seq=seq)
        else:
            h = _mixer_b(h, attn_norm[i][None, :], b_w, j, b_w_in[j], b_w_gate_f[j], b_gate_bias_f[j], b_w_gate_b[j],
                         b_gate_bias_b[j], b_out_norm[j], b_w_out[j], batch=batch, seq=seq)
        h = _ffn(h, ffn_norm[i][None, :], ffn_wgu, ffn_wd, i)
    return h.reshape(batch, seq, D_MODEL)
```

```python
import functools

import numpy as np
import jax
import jax.numpy as jnp
from jax import lax
from jax.experimental import pallas as pl
from jax.experimental.pallas import tpu as pltpu

F32 = jnp.float32
BF16 = jnp.bfloat16

D_MODEL = 1024
DEPTH = 4
RMS_EPS = 1e-6
NEG_INF = -1e30

A_GROUPS = ((128, 1), (512, 4), (2048, 16))
A_HEADS = 16
A_HEAD_DIM = 64
A_HALF = 64
ROPE_THETA = 10000.0
LOG2E = 1.4426950408889634
LN2 = 0.6931471805599453
Q_SCALE = A_HEAD_DIM ** -0.5 * LOG2E

B_HEADS = 4
B_KEY_DIM = 128
B_VAL_DIM = 256
B_QK_WIDTH = B_HEADS * B_KEY_DIM
B_V_WIDTH = B_HEADS * B_VAL_DIM
B_GATE_RANK = 16
B_GATE_TAU = 16.0
B_CHUNK = 64

FFN_HIDDEN = 2816

LANES = 128
V7X_VMEM_BYTES = 64 * 1024 * 1024
VMEM_LIMIT_BYTES = V7X_VMEM_BYTES - 8 * 1024 * 1024

Q_TILE = 128
K_TILE = Q_TILE + 2 * A_HALF
HEADS_PER_SLAB = 4
SLAB = HEADS_PER_SLAB * A_HEAD_DIM
PHASE_BLOCK = 256

for _w, _d in A_GROUPS:
    assert _w // (2 * _d) == A_HALF


def _cparams(sem):
    return pltpu.CompilerParams(dimension_semantics=sem, vmem_limit_bytes=VMEM_LIMIT_BYTES)


def _nt_dot(a, b):
    return lax.dot_general(a, b, (((1,), (1,)), ((), ())), preferred_element_type=F32)


def _dot(a, b):
    return jnp.dot(a, b, preferred_element_type=F32)


def _rms_rows(x, gain):
    ms = jnp.mean(x * x, axis=-1, keepdims=True)
    return x * lax.rsqrt(ms + RMS_EPS) * gain


def _a_in_kernel(h_ref, ng_ref, w_ref, qkg_ref, cos_ref, sin_ref, seg_ref, perm_ref, out_ref, *, tm):
    g = pl.program_id(1)
    sub = PHASE_BLOCK
    perm = perm_ref[g]
    for s in range(tm // sub):
        rows = slice(s * sub, (s + 1) * sub)
        xn = _rms_rows(h_ref[rows, :], ng_ref[...]).astype(BF16)
        x = _dot(perm, xn).astype(BF16)
        cos = cos_ref[rows, :]
        sin = sin_ref[rows, :]
        for chunk, scale in ((0, Q_SCALE), (1, 1.0)):
            acc = _dot(x, w_ref[:, chunk * D_MODEL:(chunk + 1) * D_MODEL])
            base = chunk * D_MODEL
            for pair in range(D_MODEL // (2 * SLAB)):
                ab = []
                for g4 in (2 * pair, 2 * pair + 1):
                    c0 = g4 * SLAB
                    ab.append((acc[:, c0:c0 + LANES], acc[:, c0 + LANES:c0 + SLAB]))
                sq = jnp.concatenate([a * a + b * b for a, b in ab], axis=1).astype(BF16)
                ss = _dot(sq, seg_ref[...])
                for k, (a, b) in enumerate(ab):
                    c0 = base + (2 * pair + k) * SLAB
                    r = lax.rsqrt(ss[:, k * LANES:(k + 1) * LANES] * (1.0 / A_HEAD_DIM) + RMS_EPS) * scale
                    an = a * r * qkg_ref[:, c0:c0 + LANES]
                    bn = b * r * qkg_ref[:, c0 + LANES:c0 + SLAB]
                    out_ref[rows, c0:c0 + LANES] = (an * cos - bn * sin).astype(BF16)
                    out_ref[rows, c0 + LANES:c0 + SLAB] = (bn * cos + an * sin).astype(BF16)
        out_ref[rows, 2 * D_MODEL:] = _dot(x, w_ref[:, 2 * D_MODEL:]).astype(BF16)


def _a_in_proj(h, norm_gain, w_all, layer, qk_gain, cos_t, sin_t, seg, perms, *, seq, tm=1024):
    tokens = h.shape[0]
    n_groups = w_all.shape[2] // (3 * D_MODEL)
    tm = min(tm, seq)
    pos_blocks = seq // tm
    table = pl.BlockSpec((None, tm, LANES), lambda i, g: (g, i % pos_blocks, 0))
    return pl.pallas_call(
        functools.partial(_a_in_kernel, tm=tm),
        grid=(tokens // tm, n_groups),
        in_specs=[
            pl.BlockSpec((tm, D_MODEL), lambda i, g: (i, 0)),
            pl.BlockSpec((1, D_MODEL), lambda i, g: (0, 0)),
            pl.BlockSpec((None, D_MODEL, 3 * D_MODEL), lambda i, g: (layer, 0, g)),
            pl.BlockSpec((1, 3 * D_MODEL), lambda i, g: (0, g)),
            table,
            table,
            pl.BlockSpec(seg.shape, lambda i, g: (0, 0)),
            pl.BlockSpec(perms.shape, lambda i, g: (0, 0, 0)),
        ],
        out_specs=pl.BlockSpec((tm, 3 * D_MODEL), lambda i, g: (i, g)),
        out_shape=jax.ShapeDtypeStruct((tokens, n_groups * 3 * D_MODEL), BF16),
        compiler_params=_cparams(("parallel", "parallel")),
        name="a_in_proj",
    )(h, norm_gain, w_all, qk_gain, cos_t, sin_t, seg, perms)


def _attn_kernel(q_ref, kp_ref, kc_ref, kn_ref, vp_ref, vc_ref, vn_ref, band_ref, o_ref, st_ref,
                 q2_ref, kcat_ref, vm_ref, o2_ref, st2_ref, *, tq, phase_len):
    i = pl.program_id(2)
    flat = lambda ref: ref[...].reshape(-1, ref.shape[-1])
    q2_ref[...] = flat(q_ref)
    kcat_ref[0:A_HALF, :] = flat(kp_ref)
    kcat_ref[A_HALF:A_HALF + tq, :] = flat(kc_ref)
    kcat_ref[A_HALF + tq:2 * A_HALF + tq, :] = flat(kn_ref)
    lane = lax.broadcasted_iota(jnp.int32, (1, D_MODEL), 1)
    for hi in range(HEADS_PER_SLAB):
        keep = (((lane & (SLAB - 1)) >> 6) == hi).astype(F32).astype(BF16)
        vm_ref[hi, 0:A_HALF, :] = flat(vp_ref) * keep
        vm_ref[hi, A_HALF:A_HALF + tq, :] = flat(vc_ref) * keep
        vm_ref[hi, A_HALF + tq:2 * A_HALF + tq, :] = flat(vn_ref) * keep

    stack = HEADS_PER_SLAB * Q_TILE
    key_col = lax.broadcasted_iota(jnp.int32, (1, K_TILE), 1)
    col1 = lax.broadcasted_iota(jnp.int32, (Q_TILE, K_TILE), 1)
    q_head = (col1 & (LANES - 1)) >> 5
    q_keep = [(q_head == hi).astype(F32).astype(BF16) for hi in range(HEADS_PER_SLAB)]
    st_col = lax.broadcasted_iota(jnp.int32, (Q_TILE, LANES), 1)

    def tile(s):
        r0 = s * Q_TILE
        key_pos = i * tq + s * Q_TILE - A_HALF + key_col
        in_seq = (key_pos >= 0) & (key_pos < phase_len)
        bias = band_ref[...] + jnp.where(in_seq, 0.0, NEG_INF)
        stats = jnp.zeros((Q_TILE, LANES), F32)
        for hg in range(D_MODEL // SLAB):
            cs = slice(hg * SLAB, (hg + 1) * SLAB)
            q4 = q2_ref[pl.ds(r0, Q_TILE), cs]
            k4 = kcat_ref[pl.ds(r0, K_TILE), cs]
            qm = jnp.concatenate([q4 * q_keep[hi] for hi in range(HEADS_PER_SLAB)], axis=0)
            sc = _nt_dot(qm, k4) + bias
            m = jnp.max(sc, axis=-1, keepdims=True)
            p = jnp.exp2(sc - m)
            l = jnp.sum(p, axis=-1, keepdims=True)
            pb = p.astype(BF16)
            m_rep = jnp.broadcast_to(m, (stack, LANES))
            l_rep = jnp.broadcast_to(l, (stack, LANES))
            o4 = None
            for hi in range(HEADS_PER_SLAB):
                rows = slice(hi * Q_TILE, (hi + 1) * Q_TILE)
                part = _dot(pb[rows], vm_ref[hi, pl.ds(r0, K_TILE), cs])
                o4 = part if o4 is None else o4 + part
                head = hg * HEADS_PER_SLAB + hi
                stats = jnp.where(st_col == head, m_rep[rows], stats)
                stats = jnp.where(st_col == A_HEADS + head, l_rep[rows], stats)
            o2_ref[pl.ds(r0, Q_TILE), cs] = o4.astype(BF16)
        st2_ref[pl.ds(r0, Q_TILE), :] = stats

    for s in range(tq // Q_TILE):
        tile(s)
    o_ref[...] = o2_ref[...].reshape(o_ref.shape)
    st_ref[...] = st2_ref[...].reshape(st_ref.shape)


def _attn_group(qkv, group, band, *, batch, seq, tq=512):
    dil = A_GROUPS[group][1]
    tokens, width = qkv.shape
    phase_len = seq // dil
    tq = min(tq, phase_len)
    nq = phase_len // tq
    hb = phase_len // A_HALF
    r = tq // A_HALF
    run = PHASE_BLOCK // dil
    cq = group * 3
    cur_i = lambda b, i: b * nq + i
    prev_i = lambda b, i: b * hb + jnp.maximum(i * r - 1, 0)
    next_i = lambda b, i: b * hb + jnp.minimum((i + 1) * r, hb - 1)

    if dil == 1:
        view_shape = lambda cols: (tokens, cols)
        spec = lambda rows, cols, row_i, col_i: pl.BlockSpec(
            (rows, cols), lambda b, p, i: (row_i(b, i), col_i))
    else:
        view_shape = lambda cols: (tokens // PHASE_BLOCK, dil, run, cols)
        spec = lambda rows, cols, row_i, col_i: pl.BlockSpec(
            (rows // run, None, run, cols), lambda b, p, i: (row_i(b, i), p, 0, col_i))
    cur = lambda c: spec(tq, D_MODEL, cur_i, c)
    prev = lambda c: spec(A_HALF, D_MODEL, prev_i, c)
    nxt = lambda c: spec(A_HALF, D_MODEL, next_i, c)

    qkv_v = qkv.reshape(view_shape(width))
    o, stats = pl.pallas_call(
        functools.partial(_attn_kernel, tq=tq, phase_len=phase_len),
        grid=(batch, dil, nq),
        in_specs=[cur(cq), prev(cq + 1), cur(cq + 1), nxt(cq + 1), prev(cq + 2), cur(cq + 2), nxt(cq + 2),
                  pl.BlockSpec(band.shape, lambda b, p, i: (0, 0))],
        out_specs=[spec(tq, D_MODEL, cur_i, 0), spec(tq, LANES, cur_i, 0)],
        out_shape=[
            jax.ShapeDtypeStruct(view_shape(D_MODEL), BF16),
            jax.ShapeDtypeStruct(view_shape(LANES), F32),
        ],
        scratch_shapes=[pltpu.VMEM((tq, D_MODEL), BF16),
                        pltpu.VMEM((tq + 2 * A_HALF, D_MODEL), BF16),
                        pltpu.VMEM((HEADS_PER_SLAB, tq + 2 * A_HALF, D_MODEL), BF16),
                        pltpu.VMEM((tq, D_MODEL), BF16),
                        pltpu.VMEM((tq, LANES), F32)],
        compiler_params=_cparams(("parallel", "parallel", "parallel")),
        name=f"attn_group{group}",
    )(qkv_v, qkv_v, qkv_v, qkv_v, qkv_v, qkv_v, qkv_v, band)
    return o.reshape(tokens, D_MODEL), stats.reshape(tokens, LANES)


def _a_out_kernel(o0_ref, o1_ref, o2_ref, s0_ref, s1_ref, s2_ref, exp_ref, unperm_ref, w_ref, h_ref, out_ref,
                  *, tm):
    for s in range(tm // PHASE_BLOCK):
        rows = slice(s * PHASE_BLOCK, (s + 1) * PHASE_BLOCK)
        outs = [o0_ref[rows, :].astype(F32)]
        stats = [s0_ref[rows, :]]
        for g, (o_ref, s_ref) in enumerate(((o1_ref, s1_ref), (o2_ref, s2_ref))):
            outs.append(_dot(unperm_ref[g], o_ref[rows, :]))
            st = s_ref[rows, :]
            hi = st.astype(BF16)
            lo = (st - hi.astype(F32)).astype(BF16)
            both = _dot(unperm_ref[g], jnp.concatenate([hi, lo], axis=1))
            stats.append(both[:, :LANES] + both[:, LANES:])
        is_head = lax.broadcasted_iota(jnp.int32, (PHASE_BLOCK, LANES), 1) < A_HEADS
        dens = [jnp.where(is_head, pltpu.roll(st, LANES - A_HEADS, 1), 1.0) for st in stats]
        lses = [jnp.where(is_head, st * LN2 + jnp.log(d), 0.0) for st, d in zip(stats, dens)]
        m = jnp.maximum(jnp.maximum(lses[0], lses[1]), lses[2])
        es = [jnp.exp(x - m) for x in lses]
        inv = 1.0 / (es[0] + es[1] + es[2])
        mix = None
        for e, d, o in zip(es, dens, outs):
            wgt = jnp.where(is_head, e * inv / d, 0.0)
            hi = wgt.astype(BF16)
            lo = (wgt - hi.astype(F32)).astype(BF16)
            wide = _dot(jnp.concatenate([hi, lo], axis=1), exp_ref[...])
            mix = wide * o if mix is None else mix + wide * o
        out_ref[rows, :] = h_ref[rows, :] + _dot(mix.astype(BF16), w_ref[...])


def _a_out_proj(os_, stats, expand, unperms, w_out, h, *, tm=1024):
    tokens = h.shape[0]
    row = lambda i: (i, 0)
    const = lambda i: (0, 0)
    return pl.pallas_call(
        functools.partial(_a_out_kernel, tm=tm),
        grid=(tokens // tm,),
        in_specs=[pl.BlockSpec((tm, D_MODEL), row)] * 3 + [pl.BlockSpec((tm, LANES), row)] * 3 + [
            pl.BlockSpec((2 * LANES, D_MODEL), const),
            pl.BlockSpec(unperms.shape, lambda i: (0, 0, 0)),
            pl.BlockSpec((D_MODEL, D_MODEL), const),
            pl.BlockSpec((tm, D_MODEL), row),
        ],
        out_specs=pl.BlockSpec((tm, D_MODEL), row),
        out_shape=jax.ShapeDtypeStruct((tokens, D_MODEL), F32),
        compiler_params=_cparams(("parallel",)),
        name="a_out_proj",
    )(*os_, *stats, expand, unperms, w_out, h)


def _ffn_kernel(h_ref, g_ref, wgu_ref, wd_ref, out_ref, *, th):
    x = h_ref[...]
    xn = _rms_rows(x, g_ref[...]).astype(BF16)
    acc = x
    for c in range(FFN_HIDDEN // th):
        cols = slice(c * th, (c + 1) * th)
        gate = _dot(xn, wgu_ref[:, cols])
        up = _dot(xn, wgu_ref[:, FFN_HIDDEN + c * th:FFN_HIDDEN + (c + 1) * th])
        act = (gate * jax.nn.sigmoid(gate) * up).astype(BF16)
        acc = acc + _dot(act, wd_ref[cols, :])
    out_ref[...] = acc


def _ffn(h, gain, wgu, wd, layer, *, tm=512, th=256):
    tokens = h.shape[0]
    return pl.pallas_call(
        functools.partial(_ffn_kernel, th=th),
        grid=(tokens // tm,),
        in_specs=[
            pl.BlockSpec((tm, D_MODEL), lambda i: (i, 0)),
            pl.BlockSpec((1, D_MODEL), lambda i: (0, 0)),
            pl.BlockSpec((None,) + wgu.shape[1:], lambda i: (layer, 0, 0)),
            pl.BlockSpec((None,) + wd.shape[1:], lambda i: (layer, 0, 0)),
        ],
        out_specs=pl.BlockSpec((tm, D_MODEL), lambda i: (i, 0)),
        out_shape=jax.ShapeDtypeStruct((tokens, D_MODEL), F32),
        compiler_params=_cparams(("parallel",)),
        name="ffn",
    )(h, gain, wgu, wd)


def _log_sigmoid(x):
    return jnp.minimum(x, 0.0) - jnp.log1p(jnp.exp(-jnp.abs(x)))


def _b_in_kernel(h_ref, ng_ref, wmain_ref, wz_ref, wgf_ref, wgb_ref, bf_ref, bb_ref,
                 q_ref, k_ref, v_ref, vt_ref, sr_ref, laf_ref, lab_ref, *, tm):
    sub = PHASE_BLOCK
    for s in range(tm // sub):
        rows = slice(s * sub, (s + 1) * sub)
        xn = _rms_rows(h_ref[rows, :], ng_ref[...]).astype(BF16)
        y = _dot(xn, wmain_ref[...])
        q_ref[rows, :] = (y[:, :B_QK_WIDTH] * (B_KEY_DIM ** -0.5)).astype(BF16)
        k_ref[rows, :] = y[:, B_QK_WIDTH:2 * B_QK_WIDTH].astype(BF16)
        v = y[:, 2 * B_QK_WIDTH:2 * B_QK_WIDTH + B_V_WIDTH]
        v_ref[rows, :] = v.astype(BF16)
        vt_ref[:, rows] = v.T.astype(BF16)
        r = y[:, 2 * B_QK_WIDTH + B_V_WIDTH:]
        sr_ref[rows, :] = (r * jax.nn.sigmoid(r)).astype(BF16)
        z = _dot(xn, wz_ref[...]).astype(BF16)
        laf_ref[rows, :] = _log_sigmoid(_dot(z, wgf_ref[...]) + bf_ref[...]) * (1.0 / B_GATE_TAU)
        lab_ref[rows, :] = _log_sigmoid(_dot(z, wgb_ref[...]) + bb_ref[...]) * (1.0 / B_GATE_TAU)


def _b_in_proj(h, norm_gain, w_all, layer, wz, wgf, wgb, bias_f, bias_b, *, tm=1024):
    tokens = h.shape[0]
    row = lambda i: (i, 0)
    const = lambda i: (0, 0)
    full = lambda a: pl.BlockSpec(a.shape, const)
    n_main = 2 * B_QK_WIDTH + 2 * B_V_WIDTH
    return pl.pallas_call(
        functools.partial(_b_in_kernel, tm=tm),
        grid=(tokens // tm,),
        in_specs=[pl.BlockSpec((tm, D_MODEL), row), full(norm_gain),
                  pl.BlockSpec((None, D_MODEL, n_main), lambda i: (layer, 0, 0))] + [full(a) for a in
                  (wz, wgf, wgb, bias_f, bias_b)],
        out_specs=[
            pl.BlockSpec((tm, B_QK_WIDTH), row),
            pl.BlockSpec((tm, B_QK_WIDTH), row),
            pl.BlockSpec((tm, B_V_WIDTH), row),
            pl.BlockSpec((B_V_WIDTH, tm), lambda i: (0, i)),
            pl.BlockSpec((tm, B_V_WIDTH), row),
            pl.BlockSpec((tm, B_QK_WIDTH), row),
            pl.BlockSpec((tm, B_QK_WIDTH), row),
        ],
        out_shape=[
            jax.ShapeDtypeStruct((tokens, B_QK_WIDTH), BF16),
            jax.ShapeDtypeStruct((tokens, B_QK_WIDTH), BF16),
            jax.ShapeDtypeStruct((tokens, B_V_WIDTH), BF16),
            jax.ShapeDtypeStruct((B_V_WIDTH, tokens), BF16),
            jax.ShapeDtypeStruct((tokens, B_V_WIDTH), BF16),
            jax.ShapeDtypeStruct((tokens, B_QK_WIDTH), F32),
            jax.ShapeDtypeStruct((tokens, B_QK_WIDTH), F32),
        ],
        compiler_params=_cparams(("parallel",)),
        name="b_in_proj",
    )(h, norm_gain, w_all, wz, wgf, wgb, bias_f, bias_b)


def _gla_block(q_ref, k_ref, v_ref, vt_ref, la_ref, tri_ref, st_ref, o_ref, scr, *, cb, reverse):
    qt_ref, kt_ref, ke_ref, dec_ref, kv_ref, sin_ref, att_ref = scr
    n_ch = cb // B_CHUNK
    pair_w = 2 * B_CHUNK
    row = lax.broadcasted_iota(jnp.int32, (B_CHUNK, B_CHUNK), 0)
    col = lax.broadcasted_iota(jnp.int32, (B_CHUNK, B_CHUNK), 1)
    amask = (col > row) if reverse else (col <= row)
    heads = [(slice(hh * B_KEY_DIM, (hh + 1) * B_KEY_DIM), slice(hh * B_VAL_DIM, (hh + 1) * B_VAL_DIM))
             for hh in range(B_HEADS)]
    chunk_rows = lambda c: slice(c * B_CHUNK, (c + 1) * B_CHUNK)

    for c in range(n_ch):
        rows = chunk_rows(c)
        g = la_ref[rows, :]
        g_hi = g.astype(BF16)
        g_lo = (g - g_hi.astype(F32)).astype(BF16)
        b = _dot(tri_ref[...], g_hi) + _dot(tri_ref[...], g_lo)
        dec = jnp.exp(b[0:1, :] if reverse else b[B_CHUNK - 1:B_CHUNK, :])
        dec_ref[c:c + 1, :] = dec
        qt_ref[rows, :] = (q_ref[rows, :].astype(F32) * jnp.exp(b)).astype(BF16)
        kt_f = k_ref[rows, :].astype(F32) * jnp.exp(-b)
        kt_ref[rows, :] = kt_f.astype(BF16)
        ke_ref[rows, :] = (kt_f * dec).astype(BF16)

    for c in range(n_ch):
        rows = chunk_rows(c)
        for hh, (sk, _) in enumerate(heads):
            att = jnp.where(amask, _nt_dot(qt_ref[rows, sk], kt_ref[rows, sk]), 0.0)
            att_ref[c, hh] = att.astype(BF16)

    zero = jnp.zeros((B_CHUNK, B_KEY_DIM), BF16)
    for pair in range(n_ch // 2):
        r0, r1 = chunk_rows(2 * pair), chunk_rows(2 * pair + 1)
        for hh, (sk, sv) in enumerate(heads):
            rhs = jnp.concatenate([jnp.concatenate([ke_ref[r0, sk], zero], axis=1),
                                   jnp.concatenate([zero, ke_ref[r1, sk]], axis=1)], axis=0)
            kv2 = _dot(vt_ref[sv, pair * pair_w:(pair + 1) * pair_w], rhs)
            kv_ref[2 * pair, hh] = kv2[:, :B_KEY_DIM]
            kv_ref[2 * pair + 1, hh] = kv2[:, B_KEY_DIM:]

    for step in range(n_ch):
        c = n_ch - 1 - step if reverse else step
        for hh, (sk, _) in enumerate(heads):
            st = st_ref[hh]
            sin_ref[c, hh] = st.astype(BF16)
            st_ref[hh] = st * dec_ref[c:c + 1, sk] + kv_ref[c, hh]

    for c in range(n_ch):
        rows = chunk_rows(c)
        for hh, (sk, sv) in enumerate(heads):
            o = _dot(att_ref[c, hh], v_ref[rows, sv]) + _nt_dot(qt_ref[rows, sk], sin_ref[c, hh])
            o_ref[rows, sv] = o.astype(o_ref.dtype)


def _gla_scratch(cb):
    n_ch = cb // B_CHUNK
    return [pltpu.VMEM((cb, B_QK_WIDTH), BF16), pltpu.VMEM((cb, B_QK_WIDTH), BF16),
            pltpu.VMEM((cb, B_QK_WIDTH), BF16), pltpu.VMEM((n_ch, B_QK_WIDTH), F32),
            pltpu.VMEM((n_ch, B_HEADS, B_VAL_DIM, B_KEY_DIM), F32),
            pltpu.VMEM((n_ch, B_HEADS, B_VAL_DIM, B_KEY_DIM), BF16),
            pltpu.VMEM((n_ch, B_HEADS, B_CHUNK, B_CHUNK), BF16)]


def _gla_fwd_kernel(q_ref, k_ref, v_ref, vt_ref, la_ref, tri_ref, o_ref, st_ref, *scr, cb):
    @pl.when(pl.program_id(1) == 0)
    def _():
        st_ref[...] = jnp.zeros_like(st_ref)

    _gla_block(q_ref, k_ref, v_ref, vt_ref, la_ref, tri_ref, st_ref, o_ref, scr, cb=cb, reverse=False)


def _gla_bwd_out_kernel(q_ref, k_ref, v_ref, vt_ref, la_ref, tri_ref, of_ref, sr_ref, og_ref, w_ref,
                        h_ref, out_ref, st_ref, ob_ref, *scr, cb):
    @pl.when(pl.program_id(1) == 0)
    def _():
        st_ref[...] = jnp.zeros_like(st_ref)

    _gla_block(q_ref, k_ref, v_ref, vt_ref, la_ref, tri_ref, st_ref, ob_ref, scr, cb=cb, reverse=True)
    ys = []
    for hh in range(B_HEADS):
        sv = slice(hh * B_VAL_DIM, (hh + 1) * B_VAL_DIM)
        o = of_ref[:, sv].astype(F32) + ob_ref[:, sv]
        ys.append((_rms_rows(o, og_ref[:, sv]) * sr_ref[:, sv].astype(F32)).astype(BF16))
    out_ref[...] = h_ref[...] + _dot(jnp.concatenate(ys, axis=1), w_ref[...])


def _gla_specs(batch, seq, cb, reverse):
    nb = seq // cb
    blk = (lambda b, i: b * nb + (nb - 1 - i)) if reverse else (lambda b, i: b * nb + i)
    row = lambda width: pl.BlockSpec((cb, width), lambda b, i: (blk(b, i), 0))
    const = lambda shape: pl.BlockSpec(shape, lambda b, i: (0,) * len(shape))
    vt = pl.BlockSpec((B_V_WIDTH, cb), lambda b, i: (0, blk(b, i)))
    return nb, row, const, vt


def _gla_fwd(q, k, v, vt, la, tri, *, batch, seq, cb=512):
    cb = min(cb, seq)
    nb, row, const, vt_spec = _gla_specs(batch, seq, cb, False)
    return pl.pallas_call(
        functools.partial(_gla_fwd_kernel, cb=cb),
        grid=(batch, nb),
        in_specs=[row(B_QK_WIDTH), row(B_QK_WIDTH), row(B_V_WIDTH), vt_spec, row(B_QK_WIDTH),
                  const((B_CHUNK, B_CHUNK))],
        out_specs=row(B_V_WIDTH),
        out_shape=jax.ShapeDtypeStruct((batch * seq, B_V_WIDTH), BF16),
        scratch_shapes=[pltpu.VMEM((B_HEADS, B_VAL_DIM, B_KEY_DIM), F32)] + _gla_scratch(cb),
        compiler_params=_cparams(("parallel", "arbitrary")),
        name="gla_fwd",
    )(q, k, v, vt, la, tri)


def _gla_bwd_out(q, k, v, vt, la, tri, o_f, silu_r, out_gain, w_out, h, *, batch, seq, cb=512):
    cb = min(cb, seq)
    nb, row, const, vt_spec = _gla_specs(batch, seq, cb, True)
    return pl.pallas_call(
        functools.partial(_gla_bwd_out_kernel, cb=cb),
        grid=(batch, nb),
        in_specs=[row(B_QK_WIDTH), row(B_QK_WIDTH), row(B_V_WIDTH), vt_spec, row(B_QK_WIDTH),
                  const((B_CHUNK, B_CHUNK)), row(B_V_WIDTH), row(B_V_WIDTH), const((1, B_V_WIDTH)),
                  const((B_V_WIDTH, D_MODEL)), row(D_MODEL)],
        out_specs=row(D_MODEL),
        out_shape=jax.ShapeDtypeStruct((batch * seq, D_MODEL), F32),
        scratch_shapes=[pltpu.VMEM((B_HEADS, B_VAL_DIM, B_KEY_DIM), F32),
                        pltpu.VMEM((cb, B_V_WIDTH), F32)] + _gla_scratch(cb),
        compiler_params=_cparams(("parallel", "arbitrary")),
        name="gla_bwd_out",
    )(q, k, v, vt, la, tri, o_f, silu_r, out_gain, w_out, h)


def _w_prep_kernel(w_ref, p_ref, out_ref):
    wb = w_ref[...].astype(BF16)
    is_v = pl.program_id(1) % 3 == 2

    @pl.when(is_v)
    def _():
        out_ref[...] = wb

    @pl.when(jnp.logical_not(is_v))
    def _():
        for s in range(D_MODEL // SLAB):
            cols = slice(s * SLAB, (s + 1) * SLAB)
            out_ref[:, cols] = _dot(wb[:, cols], p_ref[...]).astype(BF16)


def _a_weights(a_w_in):
    n = np.arange(SLAB)
    half, slot, j = n // LANES, (n % LANES) // 32, n % 32
    src_col = slot * A_HEAD_DIM + half * (A_HEAD_DIM // 2) + j
    p = jnp.asarray(np.arange(SLAB)[:, None] == src_col[None, :], BF16)
    layers, _, width = a_w_in.shape
    spec = pl.BlockSpec((None, D_MODEL, D_MODEL), lambda l, c: (l, 0, c))
    return pl.pallas_call(
        _w_prep_kernel,
        grid=(layers, width // D_MODEL),
        in_specs=[spec, pl.BlockSpec((SLAB, SLAB), lambda l, c: (0, 0))],
        out_specs=spec,
        out_shape=jax.ShapeDtypeStruct(a_w_in.shape, BF16),
        compiler_params=_cparams(("parallel", "parallel")),
        name="a_weight_prep",
    )(a_w_in, p)


def _a_qk_gains(q_gain, k_gain):
    n_g, n_slab, half = len(A_GROUPS), D_MODEL // SLAB, A_HEAD_DIM // 2

    def spread(gain):
        g5 = jnp.broadcast_to(gain.reshape(n_g, 1, 2, 1, half), (n_g, n_slab, 2, HEADS_PER_SLAB, half))
        return g5.reshape(n_g, 1, D_MODEL)

    gains = jnp.concatenate([spread(q_gain), spread(k_gain), jnp.ones((n_g, 1, D_MODEL), F32)], axis=1)
    return gains.reshape(1, n_g * 3 * D_MODEL).astype(F32)


def _phase_major_index(dil):
    r = np.arange(PHASE_BLOCK)
    run = PHASE_BLOCK // dil
    return (r % run) * dil + r // run


def _rope_tables(seq):
    half = A_HEAD_DIM // 2
    inv_freq = ROPE_THETA ** (-jnp.arange(half, dtype=F32) / half)
    blocks = np.arange(seq // PHASE_BLOCK)[:, None] * PHASE_BLOCK
    pos = np.stack([(blocks + _phase_major_index(d)[None, :]).reshape(seq) for _, d in A_GROUPS])
    ang = jnp.asarray(pos, F32)[:, :, None] * inv_freq[None, None, :]
    reps = LANES // half
    return jnp.concatenate([jnp.cos(ang)] * reps, axis=-1), jnp.concatenate([jnp.sin(ang)] * reps, axis=-1)


def _phase_perms():
    fwd = np.stack([np.eye(PHASE_BLOCK)[_phase_major_index(d)] for _, d in A_GROUPS])
    return jnp.asarray(fwd, BF16), jnp.asarray(fwd[1:].transpose(0, 2, 1), BF16)


def _band_bias():
    r = np.arange(HEADS_PER_SLAB * Q_TILE)[:, None] % Q_TILE
    c = np.arange(K_TILE)[None, :]
    return jnp.asarray(np.where((c >= r) & (c <= r + 2 * A_HALF), 0.0, NEG_INF), F32)


def _segment_ones():
    a = np.arange(2 * LANES)
    return jnp.asarray((a[:, None] // 32) == (a[None, :] // 32), BF16)


def _head_expand():
    a = np.arange(2 * LANES) % LANES
    n = np.arange(D_MODEL) // A_HEAD_DIM
    return jnp.asarray(a[:, None] == n[None, :], BF16)


def _tri(reverse):
    r = np.arange(B_CHUNK)
    m = (r[None, :] >= r[:, None]) if reverse else (r[None, :] <= r[:, None])
    return jnp.asarray(m, BF16)


def _b_layer_params(w_in, w_gate_f, w_gate_b):
    n_main = 2 * B_QK_WIDTH + 2 * B_V_WIDTH
    wz = jnp.pad(w_in[:, n_main:], ((0, 0), (0, LANES - 2 * B_GATE_RANK))).astype(BF16)
    wgf = jnp.pad(w_gate_f, ((0, LANES - B_GATE_RANK), (0, 0))).astype(BF16)
    wgb = jnp.pad(w_gate_b, ((B_GATE_RANK, LANES - 2 * B_GATE_RANK), (0, 0))).astype(BF16)
    return wz, wgf, wgb


def _mixer_a(h, norm_gain, w_all, layer, q_gain, k_gain, w_out, tables, *, batch, seq):
    cos_t, sin_t, seg, expand, perms, unperms, band = tables
    qk_gain = _a_qk_gains(q_gain, k_gain)
    qkv = _a_in_proj(h, norm_gain, w_all, layer, qk_gain, cos_t, sin_t, seg, perms, seq=seq)
    outs = [_attn_group(qkv, g, band, batch=batch, seq=seq) for g in range(len(A_GROUPS))]
    return _a_out_proj([o for o, _ in outs], [s for _, s in outs], expand, unperms, w_out.astype(BF16), h)


def _mixer_b(h, norm_gain, w_all, layer, w_in, w_gate_f, bias_f, w_gate_b, bias_b, out_gain, w_out, *, batch, seq):
    wz, wgf, wgb = _b_layer_params(w_in, w_gate_f, w_gate_b)
    q, k, v, vt, silu_r, la_f, la_b = _b_in_proj(h, norm_gain, w_all, layer, wz, wgf, wgb,
                                                bias_f[None, :], bias_b[None, :])
    o_f = _gla_fwd(q, k, v, vt, la_f, _tri(False), batch=batch, seq=seq)
    return _gla_bwd_out(q, k, v, vt, la_b, _tri(True), o_f, silu_r, out_gain.reshape(1, B_V_WIDTH),
                        w_out.astype(BF16), h, batch=batch, seq=seq)


def kernel(x, attn_norm, ffn_norm, a_w_in, a_q_norm, a_k_norm, a_w_out, b_w_in, b_w_gate_f, b_gate_bias_f,
           b_w_gate_b, b_gate_bias_b, b_out_norm, b_w_out, ffn_w_gate_up, ffn_w_down):
    batch, seq, _ = x.shape
    h = x.reshape(batch * seq, D_MODEL)
    ffn_wgu, ffn_wd, b_w = ffn_w_gate_up.astype(BF16), ffn_w_down.astype(BF16), b_w_in.astype(BF16)
    a_w = _a_weights(a_w_in)
    tables = _rope_tables(seq) + (_segment_ones(), _head_expand()) + _phase_perms() + (_band_bias(),)
    for i in range(DEPTH):
        j = i // 2
        if i % 2 == 0:
            h = _mixer_a(h, attn_norm[i][None, :], a_w, j, a_q_norm[j], a_k_norm[j], a_w_out[j], tables,
                         batch=batch, seq=seq)
        else:
            h = _mixer_b(h, attn_norm[i][None, :], b_w, j, b_w_in[j], b_w_gate_f[j], b_gate_bias_f[j], b_w_gate_b[j],
                         b_gate_bias_b[j], b_out_norm[j], b_w_out[j], batch=batch, seq=seq)
        h = _ffn(h, ffn_norm[i][None, :], ffn_wgu, ffn_wd, i)
    return h.reshape(batch, seq, D_MODEL)
```

```python
import functools

import numpy as np
import jax
import jax.numpy as jnp
from jax import lax
from jax.experimental import pallas as pl
from jax.experimental.pallas import tpu as pltpu

F32 = jnp.float32
BF16 = jnp.bfloat16

D_MODEL = 1024
DEPTH = 4
RMS_EPS = 1e-6
NEG_INF = -1e30

A_GROUPS = ((128, 1), (512, 4), (2048, 16))
A_HEADS = 16
A_HEAD_DIM = 64
A_HALF = 64
ROPE_THETA = 10000.0
LOG2E = 1.4426950408889634
LN2 = 0.6931471805599453
Q_SCALE = A_HEAD_DIM ** -0.5 * LOG2E

B_HEADS = 4
B_KEY_DIM = 128
B_VAL_DIM = 256
B_QK_WIDTH = B_HEADS * B_KEY_DIM
B_V_WIDTH = B_HEADS * B_VAL_DIM
B_GATE_RANK = 16
B_GATE_TAU = 16.0
B_CHUNK = 64

FFN_HIDDEN = 2816

LANES = 128
V7X_VMEM_BYTES = 64 * 1024 * 1024
VMEM_LIMIT_BYTES = V7X_VMEM_BYTES - 8 * 1024 * 1024

Q_TILE = 128
K_TILE = Q_TILE + 2 * A_HALF
HEADS_PER_SLAB = 4
SLAB = HEADS_PER_SLAB * A_HEAD_DIM
PHASE_BLOCK = 256

for _w, _d in A_GROUPS:
    assert _w // (2 * _d) == A_HALF


def _cparams(sem):
    return pltpu.CompilerParams(dimension_semantics=sem, vmem_limit_bytes=VMEM_LIMIT_BYTES)


def _nt_dot(a, b):
    return lax.dot_general(a, b, (((1,), (1,)), ((), ())), preferred_element_type=F32)


def _dot(a, b):
    return jnp.dot(a, b, preferred_element_type=F32)


def _rms_rows(x, gain):
    ms = jnp.mean(x * x, axis=-1, keepdims=True)
    return x * lax.rsqrt(ms + RMS_EPS) * gain


def _a_in_kernel(h_ref, ng_ref, w_ref, qkg_ref, cos_ref, sin_ref, seg_ref, perm_ref, out_ref, *, tm):
    g = pl.program_id(1)
    sub = PHASE_BLOCK
    perm = perm_ref[g]
    for s in range(tm // sub):
        rows = slice(s * sub, (s + 1) * sub)
        xn = _rms_rows(h_ref[rows, :], ng_ref[...]).astype(BF16)
        x = _dot(perm, xn).astype(BF16)
        reps = LANES // cos_ref.shape[-1]
        cos = jnp.concatenate([cos_ref[rows, :]] * reps, axis=1)
        sin = jnp.concatenate([sin_ref[rows, :]] * reps, axis=1)
        for chunk, scale in ((0, Q_SCALE), (1, 1.0)):
            acc = _dot(x, w_ref[:, chunk * D_MODEL:(chunk + 1) * D_MODEL])
            base = chunk * D_MODEL
            for pair in range(D_MODEL // (2 * SLAB)):
                ab = []
                for g4 in (2 * pair, 2 * pair + 1):
                    c0 = g4 * SLAB
                    ab.append((acc[:, c0:c0 + LANES], acc[:, c0 + LANES:c0 + SLAB]))
                sq = jnp.concatenate([a * a + b * b for a, b in ab], axis=1).astype(BF16)
                ss = _dot(sq, seg_ref[...])
                for k, (a, b) in enumerate(ab):
                    c0 = base + (2 * pair + k) * SLAB
                    r = lax.rsqrt(ss[:, k * LANES:(k + 1) * LANES] * (1.0 / A_HEAD_DIM) + RMS_EPS) * scale
                    an = a * r * qkg_ref[:, c0:c0 + LANES]
                    bn = b * r * qkg_ref[:, c0 + LANES:c0 + SLAB]
                    out_ref[rows, c0:c0 + LANES] = (an * cos - bn * sin).astype(BF16)
                    out_ref[rows, c0 + LANES:c0 + SLAB] = (bn * cos + an * sin).astype(BF16)
        out_ref[rows, 2 * D_MODEL:] = _dot(x, w_ref[:, 2 * D_MODEL:]).astype(BF16)


def _a_in_proj(h, norm_gain, w_all, layer, qk_gain, cos_t, sin_t, seg, perms, *, seq, tm=1024):
    tokens = h.shape[0]
    n_groups = w_all.shape[2] // (3 * D_MODEL)
    tm = min(tm, seq)
    pos_blocks = seq // tm
    table = pl.BlockSpec((None, tm, cos_t.shape[-1]), lambda i, g: (g, i % pos_blocks, 0))
    return pl.pallas_call(
        functools.partial(_a_in_kernel, tm=tm),
        grid=(tokens // tm, n_groups),
        in_specs=[
            pl.BlockSpec((tm, D_MODEL), lambda i, g: (i, 0)),
            pl.BlockSpec((1, D_MODEL), lambda i, g: (0, 0)),
            pl.BlockSpec((None, D_MODEL, 3 * D_MODEL), lambda i, g: (layer, 0, g)),
            pl.BlockSpec((1, 3 * D_MODEL), lambda i, g: (0, g)),
            table,
            table,
            pl.BlockSpec(seg.shape, lambda i, g: (0, 0)),
            pl.BlockSpec(perms.shape, lambda i, g: (0, 0, 0)),
        ],
        out_specs=pl.BlockSpec((tm, 3 * D_MODEL), lambda i, g: (i, g)),
        out_shape=jax.ShapeDtypeStruct((tokens, n_groups * 3 * D_MODEL), BF16),
        compiler_params=_cparams(("parallel", "parallel")),
        name="a_in_proj",
    )(h, norm_gain, w_all, qk_gain, cos_t, sin_t, seg, perms)


def _attn_kernel(q_ref, kp_ref, kc_ref, kn_ref, vp_ref, vc_ref, vn_ref, band_ref, o_ref, st_ref,
                 q2_ref, kcat_ref, vm_ref, o2_ref, st2_ref, *, tq, phase_len):
    i = pl.program_id(2)
    flat = lambda ref: ref[...].reshape(-1, ref.shape[-1])
    q2_ref[...] = flat(q_ref)
    kcat_ref[0:A_HALF, :] = flat(kp_ref)
    kcat_ref[A_HALF:A_HALF + tq, :] = flat(kc_ref)
    kcat_ref[A_HALF + tq:2 * A_HALF + tq, :] = flat(kn_ref)
    lane = lax.broadcasted_iota(jnp.int32, (1, D_MODEL), 1)
    for hi in range(HEADS_PER_SLAB):
        keep = (((lane & (SLAB - 1)) >> 6) == hi).astype(F32).astype(BF16)
        vm_ref[hi, 0:A_HALF, :] = flat(vp_ref) * keep
        vm_ref[hi, A_HALF:A_HALF + tq, :] = flat(vc_ref) * keep
        vm_ref[hi, A_HALF + tq:2 * A_HALF + tq, :] = flat(vn_ref) * keep

    stack = HEADS_PER_SLAB * Q_TILE
    key_col = lax.broadcasted_iota(jnp.int32, (1, K_TILE), 1)
    col1 = lax.broadcasted_iota(jnp.int32, (Q_TILE, K_TILE), 1)
    q_head = (col1 & (LANES - 1)) >> 5
    q_keep = [(q_head == hi).astype(F32).astype(BF16) for hi in range(HEADS_PER_SLAB)]
    st_col = lax.broadcasted_iota(jnp.int32, (Q_TILE, LANES), 1)

    def tile(s):
        r0 = s * Q_TILE
        key_pos = i * tq + s * Q_TILE - A_HALF + key_col
        in_seq = (key_pos >= 0) & (key_pos < phase_len)
        bias = band_ref[...] + jnp.where(in_seq, 0.0, NEG_INF)
        stats = jnp.zeros((Q_TILE, LANES), F32)
        for hg in range(D_MODEL // SLAB):
            cs = slice(hg * SLAB, (hg + 1) * SLAB)
            q4 = q2_ref[pl.ds(r0, Q_TILE), cs]
            k4 = kcat_ref[pl.ds(r0, K_TILE), cs]
            qm = jnp.concatenate([q4 * q_keep[hi] for hi in range(HEADS_PER_SLAB)], axis=0)
            sc = _nt_dot(qm, k4) + bias
            m = jnp.max(sc, axis=-1, keepdims=True)
            p = jnp.exp2(sc - m)
            l = jnp.sum(p, axis=-1, keepdims=True)
            pb = p.astype(BF16)
            m_rep = jnp.broadcast_to(m, (stack, LANES))
            l_rep = jnp.broadcast_to(l, (stack, LANES))
            o4 = None
            for hi in range(HEADS_PER_SLAB):
                rows = slice(hi * Q_TILE, (hi + 1) * Q_TILE)
                part = _dot(pb[rows], vm_ref[hi, pl.ds(r0, K_TILE), cs])
                o4 = part if o4 is None else o4 + part
                head = hg * HEADS_PER_SLAB + hi
                stats = jnp.where(st_col == head, m_rep[rows], stats)
                stats = jnp.where(st_col == A_HEADS + head, l_rep[rows], stats)
            o2_ref[pl.ds(r0, Q_TILE), cs] = o4.astype(BF16)
        st2_ref[pl.ds(r0, Q_TILE), :] = stats

    for s in range(tq // Q_TILE):
        tile(s)
    o_ref[...] = o2_ref[...].reshape(o_ref.shape)
    st_ref[...] = st2_ref[...].reshape(st_ref.shape)


def _attn_group(qkv, group, band, *, batch, seq, tq=512):
    dil = A_GROUPS[group][1]
    tokens, width = qkv.shape
    phase_len = seq // dil
    tq = min(tq, phase_len)
    nq = phase_len // tq
    hb = phase_len // A_HALF
    r = tq // A_HALF
    run = PHASE_BLOCK // dil
    cq = group * 3
    cur_i = lambda b, i: b * nq + i
    prev_i = lambda b, i: b * hb + jnp.maximum(i * r - 1, 0)
    next_i = lambda b, i: b * hb + jnp.minimum((i + 1) * r, hb - 1)

    if dil == 1:
        view_shape = lambda cols: (tokens, cols)
        spec = lambda rows, cols, row_i, col_i: pl.BlockSpec(
            (rows, cols), lambda b, p, i: (row_i(b, i), col_i))
    else:
        view_shape = lambda cols: (tokens // PHASE_BLOCK, dil, run, cols)
        spec = lambda rows, cols, row_i, col_i: pl.BlockSpec(
            (rows // run, None, run, cols), lambda b, p, i: (row_i(b, i), p, 0, col_i))
    cur = lambda c: spec(tq, D_MODEL, cur_i, c)
    prev = lambda c: spec(A_HALF, D_MODEL, prev_i, c)
    nxt = lambda c: spec(A_HALF, D_MODEL, next_i, c)

    qkv_v = qkv.reshape(view_shape(width))
    o, stats = pl.pallas_call(
        functools.partial(_attn_kernel, tq=tq, phase_len=phase_len),
        grid=(batch, dil, nq),
        in_specs=[cur(cq), prev(cq + 1), cur(cq + 1), nxt(cq + 1), prev(cq + 2), cur(cq + 2), nxt(cq + 2),
                  pl.BlockSpec(band.shape, lambda b, p, i: (0, 0))],
        out_specs=[spec(tq, D_MODEL, cur_i, 0), spec(tq, LANES, cur_i, 0)],
        out_shape=[
            jax.ShapeDtypeStruct(view_shape(D_MODEL), BF16),
            jax.ShapeDtypeStruct(view_shape(LANES), F32),
        ],
        scratch_shapes=[pltpu.VMEM((tq, D_MODEL), BF16),
                        pltpu.VMEM((tq + 2 * A_HALF, D_MODEL), BF16),
                        pltpu.VMEM((HEADS_PER_SLAB, tq + 2 * A_HALF, D_MODEL), BF16),
                        pltpu.VMEM((tq, D_MODEL), BF16),
                        pltpu.VMEM((tq, LANES), F32)],
        compiler_params=_cparams(("parallel", "parallel", "parallel")),
        name=f"attn_group{group}",
    )(qkv_v, qkv_v, qkv_v, qkv_v, qkv_v, qkv_v, qkv_v, band)
    return o.reshape(tokens, D_MODEL), stats.reshape(tokens, LANES)


def _a_out_kernel(o0_ref, o1_ref, o2_ref, s0_ref, s1_ref, s2_ref, exp_ref, unperm_ref, w_ref, h_ref, out_ref,
                  *, tm):
    for s in range(tm // PHASE_BLOCK):
        rows = slice(s * PHASE_BLOCK, (s + 1) * PHASE_BLOCK)
        outs = [o0_ref[rows, :].astype(F32)]
        stats = [s0_ref[rows, :]]
        for g, (o_ref, s_ref) in enumerate(((o1_ref, s1_ref), (o2_ref, s2_ref))):
            outs.append(_dot(unperm_ref[g], o_ref[rows, :]))
            st = s_ref[rows, :]
            hi = st.astype(BF16)
            lo = (st - hi.astype(F32)).astype(BF16)
            both = _dot(unperm_ref[g], jnp.concatenate([hi, lo], axis=1))
            stats.append(both[:, :LANES] + both[:, LANES:])
        is_head = lax.broadcasted_iota(jnp.int32, (PHASE_BLOCK, LANES), 1) < A_HEADS
        dens = [jnp.where(is_head, pltpu.roll(st, LANES - A_HEADS, 1), 1.0) for st in stats]
        lses = [jnp.where(is_head, st * LN2 + jnp.log(d), 0.0) for st, d in zip(stats, dens)]
        m = jnp.maximum(jnp.maximum(lses[0], lses[1]), lses[2])
        es = [jnp.exp(x - m) for x in lses]
        inv = 1.0 / (es[0] + es[1] + es[2])
        mix = None
        for e, d, o in zip(es, dens, outs):
            wgt = jnp.where(is_head, e * inv / d, 0.0)
            hi = wgt.astype(BF16)
            lo = (wgt - hi.astype(F32)).astype(BF16)
            wide = _dot(jnp.concatenate([hi, lo], axis=1), exp_ref[...])
            mix = wide * o if mix is None else mix + wide * o
        out_ref[rows, :] = h_ref[rows, :] + _dot(mix.astype(BF16), w_ref[...])


def _a_out_proj(os_, stats, expand, unperms, w_out, h, *, tm=1024):
    tokens = h.shape[0]
    row = lambda i: (i, 0)
    const = lambda i: (0, 0)
    return pl.pallas_call(
        functools.partial(_a_out_kernel, tm=tm),
        grid=(tokens // tm,),
        in_specs=[pl.BlockSpec((tm, D_MODEL), row)] * 3 + [pl.BlockSpec((tm, LANES), row)] * 3 + [
            pl.BlockSpec((2 * LANES, D_MODEL), const),
            pl.BlockSpec(unperms.shape, lambda i: (0, 0, 0)),
            pl.BlockSpec((D_MODEL, D_MODEL), const),
            pl.BlockSpec((tm, D_MODEL), row),
        ],
        out_specs=pl.BlockSpec((tm, D_MODEL), row),
        out_shape=jax.ShapeDtypeStruct((tokens, D_MODEL), F32),
        compiler_params=_cparams(("parallel",)),
        name="a_out_proj",
    )(*os_, *stats, expand, unperms, w_out, h)


def _ffn_kernel(h_ref, g_ref, wgu_ref, wd_ref, out_ref, *, th):
    x = h_ref[...]
    xn = _rms_rows(x, g_ref[...]).astype(BF16)
    acc = x
    for c in range(FFN_HIDDEN // th):
        cols = slice(c * th, (c + 1) * th)
        gate = _dot(xn, wgu_ref[:, cols])
        up = _dot(xn, wgu_ref[:, FFN_HIDDEN + c * th:FFN_HIDDEN + (c + 1) * th])
        act = (gate * jax.nn.sigmoid(gate) * up).astype(BF16)
        acc = acc + _dot(act, wd_ref[cols, :])
    out_ref[...] = acc


def _ffn(h, gain, wgu, wd, layer, *, tm=512, th=256):
    tokens = h.shape[0]
    return pl.pallas_call(
        functools.partial(_ffn_kernel, th=th),
        grid=(tokens // tm,),
        in_specs=[
            pl.BlockSpec((tm, D_MODEL), lambda i: (i, 0)),
            pl.BlockSpec((1, D_MODEL), lambda i: (0, 0)),
            pl.BlockSpec((None,) + wgu.shape[1:], lambda i: (layer, 0, 0)),
            pl.BlockSpec((None,) + wd.shape[1:], lambda i: (layer, 0, 0)),
        ],
        out_specs=pl.BlockSpec((tm, D_MODEL), lambda i: (i, 0)),
        out_shape=jax.ShapeDtypeStruct((tokens, D_MODEL), F32),
        compiler_params=_cparams(("parallel",)),
        name="ffn",
    )(h, gain, wgu, wd)


def _log_sigmoid(x):
    return jnp.minimum(x, 0.0) - jnp.log1p(jnp.exp(-jnp.abs(x)))


def _b_in_kernel(h_ref, ng_ref, wmain_ref, wz_ref, wgf_ref, wgb_ref, bf_ref, bb_ref,
                 q_ref, k_ref, v_ref, vt_ref, sr_ref, laf_ref, lab_ref, *, tm):
    sub = PHASE_BLOCK
    for s in range(tm // sub):
        rows = slice(s * sub, (s + 1) * sub)
        xn = _rms_rows(h_ref[rows, :], ng_ref[...]).astype(BF16)
        y = _dot(xn, wmain_ref[...])
        q_ref[rows, :] = (y[:, :B_QK_WIDTH] * (B_KEY_DIM ** -0.5)).astype(BF16)
        k_ref[rows, :] = y[:, B_QK_WIDTH:2 * B_QK_WIDTH].astype(BF16)
        v = y[:, 2 * B_QK_WIDTH:2 * B_QK_WIDTH + B_V_WIDTH]
        v_ref[rows, :] = v.astype(BF16)
        vt_ref[:, rows] = v.T.astype(BF16)
        r = y[:, 2 * B_QK_WIDTH + B_V_WIDTH:]
        sr_ref[rows, :] = (r * jax.nn.sigmoid(r)).astype(BF16)
        z = _dot(xn, wz_ref[...]).astype(BF16)
        laf_ref[rows, :] = _log_sigmoid(_dot(z, wgf_ref[...]) + bf_ref[...]) * (1.0 / B_GATE_TAU)
        lab_ref[rows, :] = _log_sigmoid(_dot(z, wgb_ref[...]) + bb_ref[...]) * (1.0 / B_GATE_TAU)


def _b_in_proj(h, norm_gain, w_all, layer, wz, wgf, wgb, bias_f, bias_b, *, tm=1024):
    tokens = h.shape[0]
    row = lambda i: (i, 0)
    const = lambda i: (0, 0)
    full = lambda a: pl.BlockSpec(a.shape, const)
    n_main = 2 * B_QK_WIDTH + 2 * B_V_WIDTH
    return pl.pallas_call(
        functools.partial(_b_in_kernel, tm=tm),
        grid=(tokens // tm,),
        in_specs=[pl.BlockSpec((tm, D_MODEL), row), full(norm_gain),
                  pl.BlockSpec((None, D_MODEL, n_main), lambda i: (layer, 0, 0))] + [full(a) for a in
                  (wz, wgf, wgb, bias_f, bias_b)],
        out_specs=[
            pl.BlockSpec((tm, B_QK_WIDTH), row),
            pl.BlockSpec((tm, B_QK_WIDTH), row),
            pl.BlockSpec((tm, B_V_WIDTH), row),
            pl.BlockSpec((B_V_WIDTH, tm), lambda i: (0, i)),
            pl.BlockSpec((tm, B_V_WIDTH), row),
            pl.BlockSpec((tm, B_QK_WIDTH), row),
            pl.BlockSpec((tm, B_QK_WIDTH), row),
        ],
        out_shape=[
            jax.ShapeDtypeStruct((tokens, B_QK_WIDTH), BF16),
            jax.ShapeDtypeStruct((tokens, B_QK_WIDTH), BF16),
            jax.ShapeDtypeStruct((tokens, B_V_WIDTH), BF16),
            jax.ShapeDtypeStruct((B_V_WIDTH, tokens), BF16),
            jax.ShapeDtypeStruct((tokens, B_V_WIDTH), BF16),
            jax.ShapeDtypeStruct((tokens, B_QK_WIDTH), F32),
            jax.ShapeDtypeStruct((tokens, B_QK_WIDTH), F32),
        ],
        compiler_params=_cparams(("parallel",)),
        name="b_in_proj",
    )(h, norm_gain, w_all, wz, wgf, wgb, bias_f, bias_b)


def _gla_block(q_ref, k_ref, v_ref, vt_ref, la_ref, tri_ref, st_ref, o_ref, scr, *, cb, reverse):
    qt_ref, kt_ref, ke_ref, dec_ref, kv_ref, sin_ref, att_ref = scr
    n_ch = cb // B_CHUNK
    pair_w = 2 * B_CHUNK
    row = lax.broadcasted_iota(jnp.int32, (B_CHUNK, B_CHUNK), 0)
    col = lax.broadcasted_iota(jnp.int32, (B_CHUNK, B_CHUNK), 1)
    amask = (col > row) if reverse else (col <= row)
    heads = [(slice(hh * B_KEY_DIM, (hh + 1) * B_KEY_DIM), slice(hh * B_VAL_DIM, (hh + 1) * B_VAL_DIM))
             for hh in range(B_HEADS)]
    chunk_rows = lambda c: slice(c * B_CHUNK, (c + 1) * B_CHUNK)

    for c in range(n_ch):
        rows = chunk_rows(c)
        g = la_ref[rows, :]
        g_hi = g.astype(BF16)
        g_lo = (g - g_hi.astype(F32)).astype(BF16)
        b = _dot(tri_ref[...], g_hi) + _dot(tri_ref[...], g_lo)
        dec = jnp.exp(b[0:1, :] if reverse else b[B_CHUNK - 1:B_CHUNK, :])
        dec_ref[c:c + 1, :] = dec
        qt_ref[rows, :] = (q_ref[rows, :].astype(F32) * jnp.exp(b)).astype(BF16)
        kt_f = k_ref[rows, :].astype(F32) * jnp.exp(-b)
        kt_ref[rows, :] = kt_f.astype(BF16)
        ke_ref[rows, :] = (kt_f * dec).astype(BF16)

    for c in range(n_ch):
        rows = chunk_rows(c)
        for hh, (sk, _) in enumerate(heads):
            att = jnp.where(amask, _nt_dot(qt_ref[rows, sk], kt_ref[rows, sk]), 0.0)
            att_ref[c, hh] = att.astype(BF16)

    zero = jnp.zeros((B_CHUNK, B_KEY_DIM), BF16)
    for pair in range(n_ch // 2):
        r0, r1 = chunk_rows(2 * pair), chunk_rows(2 * pair + 1)
        for hh, (sk, sv) in enumerate(heads):
            rhs = jnp.concatenate([jnp.concatenate([ke_ref[r0, sk], zero], axis=1),
                                   jnp.concatenate([zero, ke_ref[r1, sk]], axis=1)], axis=0)
            kv2 = _dot(vt_ref[sv, pair * pair_w:(pair + 1) * pair_w], rhs)
            kv_ref[2 * pair, hh] = kv2[:, :B_KEY_DIM]
            kv_ref[2 * pair + 1, hh] = kv2[:, B_KEY_DIM:]

    for step in range(n_ch):
        c = n_ch - 1 - step if reverse else step
        for hh, (sk, _) in enumerate(heads):
            st = st_ref[hh]
            sin_ref[c, hh] = st.astype(BF16)
            st_ref[hh] = st * dec_ref[c:c + 1, sk] + kv_ref[c, hh]

    for c in range(n_ch):
        rows = chunk_rows(c)
        for hh, (sk, sv) in enumerate(heads):
            o = _dot(att_ref[c, hh], v_ref[rows, sv]) + _nt_dot(qt_ref[rows, sk], sin_ref[c, hh])
            o_ref[rows, sv] = o.astype(o_ref.dtype)


def _gla_scratch(cb):
    n_ch = cb // B_CHUNK
    return [pltpu.VMEM((cb, B_QK_WIDTH), BF16), pltpu.VMEM((cb, B_QK_WIDTH), BF16),
            pltpu.VMEM((cb, B_QK_WIDTH), BF16), pltpu.VMEM((n_ch, B_QK_WIDTH), F32),
            pltpu.VMEM((n_ch, B_HEADS, B_VAL_DIM, B_KEY_DIM), F32),
            pltpu.VMEM((n_ch, B_HEADS, B_VAL_DIM, B_KEY_DIM), BF16),
            pltpu.VMEM((n_ch, B_HEADS, B_CHUNK, B_CHUNK), BF16)]


def _gla_fwd_kernel(q_ref, k_ref, v_ref, vt_ref, la_ref, tri_ref, o_ref, st_ref, *scr, cb):
    @pl.when(pl.program_id(1) == 0)
    def _():
        st_ref[...] = jnp.zeros_like(st_ref)

    _gla_block(q_ref, k_ref, v_ref, vt_ref, la_ref, tri_ref, st_ref, o_ref, scr, cb=cb, reverse=False)


def _gla_bwd_out_kernel(q_ref, k_ref, v_ref, vt_ref, la_ref, tri_ref, of_ref, sr_ref, og_ref, w_ref,
                        h_ref, out_ref, st_ref, ob_ref, *scr, cb):
    @pl.when(pl.program_id(1) == 0)
    def _():
        st_ref[...] = jnp.zeros_like(st_ref)

    _gla_block(q_ref, k_ref, v_ref, vt_ref, la_ref, tri_ref, st_ref, ob_ref, scr, cb=cb, reverse=True)
    ys = []
    for hh in range(B_HEADS):
        sv = slice(hh * B_VAL_DIM, (hh + 1) * B_VAL_DIM)
        o = of_ref[:, sv].astype(F32) + ob_ref[:, sv]
        ys.append((_rms_rows(o, og_ref[:, sv]) * sr_ref[:, sv].astype(F32)).astype(BF16))
    out_ref[...] = h_ref[...] + _dot(jnp.concatenate(ys, axis=1), w_ref[...])


def _gla_specs(batch, seq, cb, reverse):
    nb = seq // cb
    blk = (lambda b, i: b * nb + (nb - 1 - i)) if reverse else (lambda b, i: b * nb + i)
    row = lambda width: pl.BlockSpec((cb, width), lambda b, i: (blk(b, i), 0))
    const = lambda shape: pl.BlockSpec(shape, lambda b, i: (0,) * len(shape))
    vt = pl.BlockSpec((B_V_WIDTH, cb), lambda b, i: (0, blk(b, i)))
    return nb, row, const, vt


def _gla_fwd(q, k, v, vt, la, tri, *, batch, seq, cb=512):
    cb = min(cb, seq)
    nb, row, const, vt_spec = _gla_specs(batch, seq, cb, False)
    return pl.pallas_call(
        functools.partial(_gla_fwd_kernel, cb=cb),
        grid=(batch, nb),
        in_specs=[row(B_QK_WIDTH), row(B_QK_WIDTH), row(B_V_WIDTH), vt_spec, row(B_QK_WIDTH),
                  const((B_CHUNK, B_CHUNK))],
        out_specs=row(B_V_WIDTH),
        out_shape=jax.ShapeDtypeStruct((batch * seq, B_V_WIDTH), BF16),
        scratch_shapes=[pltpu.VMEM((B_HEADS, B_VAL_DIM, B_KEY_DIM), F32)] + _gla_scratch(cb),
        compiler_params=_cparams(("parallel", "arbitrary")),
        name="gla_fwd",
    )(q, k, v, vt, la, tri)


def _gla_bwd_out(q, k, v, vt, la, tri, o_f, silu_r, out_gain, w_out, h, *, batch, seq, cb=512):
    cb = min(cb, seq)
    nb, row, const, vt_spec = _gla_specs(batch, seq, cb, True)
    return pl.pallas_call(
        functools.partial(_gla_bwd_out_kernel, cb=cb),
        grid=(batch, nb),
        in_specs=[row(B_QK_WIDTH), row(B_QK_WIDTH), row(B_V_WIDTH), vt_spec, row(B_QK_WIDTH),
                  const((B_CHUNK, B_CHUNK)), row(B_V_WIDTH), row(B_V_WIDTH), const((1, B_V_WIDTH)),
                  const((B_V_WIDTH, D_MODEL)), row(D_MODEL)],
        out_specs=row(D_MODEL),
        out_shape=jax.ShapeDtypeStruct((batch * seq, D_MODEL), F32),
        scratch_shapes=[pltpu.VMEM((B_HEADS, B_VAL_DIM, B_KEY_DIM), F32),
                        pltpu.VMEM((cb, B_V_WIDTH), F32)] + _gla_scratch(cb),
        compiler_params=_cparams(("parallel", "arbitrary")),
        name="gla_bwd_out",
    )(q, k, v, vt, la, tri, o_f, silu_r, out_gain, w_out, h)


def _w_prep_kernel(w_ref, p_ref, out_ref):
    wb = w_ref[...].astype(BF16)
    is_v = pl.program_id(1) % 3 == 2

    @pl.when(is_v)
    def _():
        out_ref[...] = wb

    @pl.when(jnp.logical_not(is_v))
    def _():
        for s in range(D_MODEL // SLAB):
            cols = slice(s * SLAB, (s + 1) * SLAB)
            out_ref[:, cols] = _dot(wb[:, cols], p_ref[...]).astype(BF16)


def _a_weights(a_w_in):
    n = np.arange(SLAB)
    half, slot, j = n // LANES, (n % LANES) // 32, n % 32
    src_col = slot * A_HEAD_DIM + half * (A_HEAD_DIM // 2) + j
    p = jnp.asarray(np.arange(SLAB)[:, None] == src_col[None, :], BF16)
    layers, _, width = a_w_in.shape
    spec = pl.BlockSpec((None, D_MODEL, D_MODEL), lambda l, c: (l, 0, c))
    return pl.pallas_call(
        _w_prep_kernel,
        grid=(layers, width // D_MODEL),
        in_specs=[spec, pl.BlockSpec((SLAB, SLAB), lambda l, c: (0, 0))],
        out_specs=spec,
        out_shape=jax.ShapeDtypeStruct(a_w_in.shape, BF16),
        compiler_params=_cparams(("parallel", "parallel")),
        name="a_weight_prep",
    )(a_w_in, p)


def _a_qk_gains(q_gain, k_gain):
    n_g, n_slab, half = len(A_GROUPS), D_MODEL // SLAB, A_HEAD_DIM // 2

    def spread(gain):
        g5 = jnp.broadcast_to(gain.reshape(n_g, 1, 2, 1, half), (n_g, n_slab, 2, HEADS_PER_SLAB, half))
        return g5.reshape(n_g, 1, D_MODEL)

    gains = jnp.concatenate([spread(q_gain), spread(k_gain), jnp.ones((n_g, 1, D_MODEL), F32)], axis=1)
    return gains.reshape(1, n_g * 3 * D_MODEL).astype(F32)


def _phase_major_index(dil):
    r = np.arange(PHASE_BLOCK)
    run = PHASE_BLOCK // dil
    return (r % run) * dil + r // run


def _rope_tables(seq):
    half = A_HEAD_DIM // 2
    inv_freq = ROPE_THETA ** (-jnp.arange(half, dtype=F32) / half)
    blocks = np.arange(seq // PHASE_BLOCK)[:, None] * PHASE_BLOCK
    pos = np.stack([(blocks + _phase_major_index(d)[None, :]).reshape(seq) for _, d in A_GROUPS])
    ang = jnp.asarray(pos, F32)[:, :, None] * inv_freq[None, None, :]
    return jnp.cos(ang), jnp.sin(ang)


def _phase_perms():
    fwd = np.stack([np.eye(PHASE_BLOCK)[_phase_major_index(d)] for _, d in A_GROUPS])
    return jnp.asarray(fwd, BF16), jnp.asarray(fwd[1:].transpose(0, 2, 1), BF16)


def _band_bias():
    r = np.arange(HEADS_PER_SLAB * Q_TILE)[:, None] % Q_TILE
    c = np.arange(K_TILE)[None, :]
    return jnp.asarray(np.where((c >= r) & (c <= r + 2 * A_HALF), 0.0, NEG_INF), F32)


def _segment_ones():
    a = np.arange(2 * LANES)
    return jnp.asarray((a[:, None] // 32) == (a[None, :] // 32), BF16)


def _head_expand():
    a = np.arange(2 * LANES) % LANES
    n = np.arange(D_MODEL) // A_HEAD_DIM
    return jnp.asarray(a[:, None] == n[None, :], BF16)


def _tri(reverse):
    r = np.arange(B_CHUNK)
    m = (r[None, :] >= r[:, None]) if reverse else (r[None, :] <= r[:, None])
    return jnp.asarray(m, BF16)


def _b_layer_params(w_in, w_gate_f, w_gate_b):
    n_main = 2 * B_QK_WIDTH + 2 * B_V_WIDTH
    wz = jnp.pad(w_in[:, n_main:], ((0, 0), (0, LANES - 2 * B_GATE_RANK))).astype(BF16)
    wgf = jnp.pad(w_gate_f, ((0, LANES - B_GATE_RANK), (0, 0))).astype(BF16)
    wgb = jnp.pad(w_gate_b, ((B_GATE_RANK, LANES - 2 * B_GATE_RANK), (0, 0))).astype(BF16)
    return wz, wgf, wgb


def _mixer_a(h, norm_gain, w_all, layer, q_gain, k_gain, w_out, tables, *, batch, seq):
    cos_t, sin_t, seg, expand, perms, unperms, band = tables
    qk_gain = _a_qk_gains(q_gain, k_gain)
    qkv = _a_in_proj(h, norm_gain, w_all, layer, qk_gain, cos_t, sin_t, seg, perms, seq=seq)
    outs = [_attn_group(qkv, g, band, batch=batch, seq=seq) for g in range(len(A_GROUPS))]
    return _a_out_proj([o for o, _ in outs], [s for _, s in outs], expand, unperms, w_out.astype(BF16), h)


def _mixer_b(h, norm_gain, w_all, layer, w_in, w_gate_f, bias_f, w_gate_b, bias_b, out_gain, w_out, *, batch, seq):
    wz, wgf, wgb = _b_layer_params(w_in, w_gate_f, w_gate_b)
    q, k, v, vt, silu_r, la_f, la_b = _b_in_proj(h, norm_gain, w_all, layer, wz, wgf, wgb,
                                                bias_f[None, :], bias_b[None, :])
    o_f = _gla_fwd(q, k, v, vt, la_f, _tri(False), batch=batch, seq=seq)
    return _gla_bwd_out(q, k, v, vt, la_b, _tri(True), o_f, silu_r, out_gain.reshape(1, B_V_WIDTH),
                        w_out.astype(BF16), h, batch=batch, seq=seq)


def kernel(x, attn_norm, ffn_norm, a_w_in, a_q_norm, a_k_norm, a_w_out, b_w_in, b_w_gate_f, b_gate_bias_f,
           b_w_gate_b, b_gate_bias_b, b_out_norm, b_w_out, ffn_w_gate_up, ffn_w_down):
    batch, seq, _ = x.shape
    h = x.reshape(batch * seq, D_MODEL)
    ffn_wgu, ffn_wd = ffn_w_gate_up.astype(BF16), ffn_w_down.astype(BF16)
    b_w = b_w_in[:, :, :2 * B_QK_WIDTH + 2 * B_V_WIDTH].astype(BF16)
    a_w = _a_weights(a_w_in)
    tables = _rope_tables(seq) + (_segment_ones(), _head_expand()) + _phase_perms() + (_band_bias(),)
    for i in range(DEPTH):
        j = i // 2
        if i % 2 == 0:
            h = _mixer_a(h, attn_norm[i][None, :], a_w, j, a_q_norm[j], a_k_norm[j], a_w_out[j], tables,
                         batch=batch, seq=seq)
        else:
            h = _mixer_b(h, attn_norm[i][None, :], b_w, j, b_w_in[j], b_w_gate_f[j], b_gate_bias_f[j], b_w_gate_b[j],
                         b_gate_bias_b[j], b_out_norm[j], b_w_out[j], batch=batch, seq=seq)
        h = _ffn(h, ffn_norm[i][None, :], ffn_wgu, ffn_wd, i)
    return h.reshape(batch, seq, D_MODEL)
```
